```python
import math
import jax, jax.numpy as jnp
from jax import lax
import numpy as np

D_MODEL = 1024
BATCH = 2
SEQ = 8192
DEPTH = 2
DEC_BATCH = 128
DEC_SEQ = 1
PAST_LEN = 2048
PAGE_SIZE = 128

HEAD_DIM = 64
N_HEADS_A = 8
N_HEADS_B = 8
N_HEADS_C = 16
IDX_HEADS = 8
IDX_DIM = 64
DSA_TOPK = 256
MOBA_BLOCK = 256
MOBA_TOPK = 3
Q_BLOCK = 128
ROPE_THETA = 500000.0
N_EXPERTS = 16
N_GROUPS = 4
EXPERTS_PER_GROUP = N_EXPERTS // N_GROUPS
TOPK_EXPERTS = 2
D_EXPERT = 512
ALPHA = (2 * DEPTH) ** 0.25
BETA = (8 * DEPTH) ** -0.25
LN_EPS = 1e-5
AB_WIDTH = 3 * N_HEADS_A * HEAD_DIM + 3 * N_HEADS_B * HEAD_DIM + IDX_HEADS * IDX_DIM + IDX_DIM + IDX_HEADS
FOX_WIDTH = 3 * N_HEADS_C * HEAD_DIM + N_HEADS_C

kernel_name = "dsa_moba_fox_grouped_moe_deepnorm_step"

F32 = jnp.float32


def layer_norm(x, g, b):
    xf = x.astype(F32)
    mu = xf.mean(-1, keepdims=True)
    var = jnp.square(xf - mu).mean(-1, keepdims=True)
    return ((xf - mu) * lax.rsqrt(var + LN_EPS) * g.astype(F32) + b.astype(F32)).astype(x.dtype)


def partial_rotary(x, pos):
    rot = x.shape[-1] // 4
    half = rot // 2
    inv_freq = ROPE_THETA ** (-jnp.arange(half, dtype=F32) / half)
    ang = pos.astype(F32)[:, None] * inv_freq[None, :]
    cos = jnp.cos(ang)[:, None, :]
    sin = jnp.sin(ang)[:, None, :]
    x1 = x[..., :half].astype(F32)
    x2 = x[..., half:rot].astype(F32)
    r1 = (x1 * cos - x2 * sin).astype(x.dtype)
    r2 = (x2 * cos + x1 * sin).astype(x.dtype)
    return jnp.concatenate([r1, r2, x[..., rot:]], axis=-1)


def gather_pages(cache, page_table):
    pages = cache[page_table]
    return pages.reshape(page_table.shape[0], page_table.shape[1] * cache.shape[1], *cache.shape[2:])


def sweep_query_blocks(fn, q_arrays, pos):
    T = pos.shape[0]
    nb = T // Q_BLOCK

    def split(a):
        return jnp.moveaxis(a.reshape(a.shape[0], nb, Q_BLOCK, *a.shape[2:]), 1, 0)

    xs = tuple(split(a) for a in q_arrays) + (pos.reshape(nb, Q_BLOCK),)
    out = lax.map(lambda args: fn(*args), xs)
    out = jnp.moveaxis(out, 0, 1)
    return out.reshape(out.shape[0], T, *out.shape[3:])


def dsa_attend(q, k, v, q_idx, k_idx, w_idx, q_pos):
    B, L = k.shape[0], k.shape[1]
    topk = min(DSA_TOPK, L // 4)
    dots = jnp.einsum('bqhd,bkd->bqhk', q_idx, k_idx).astype(F32) * (IDX_DIM ** -0.5)
    score = jnp.einsum('bqh,bqhk->bqk', w_idx.astype(F32), jax.nn.relu(dots))
    kpos = jnp.arange(L, dtype=jnp.int32)
    admissible = kpos[None, :] <= q_pos[:, None]
    score = jnp.where(admissible[None], score, -jnp.inf)
    _, sel = lax.top_k(score, topk)
    valid = sel <= q_pos[None, :, None]
    bi = jnp.arange(B)[:, None, None]
    k_sel = k[bi, sel]
    v_sel = v[bi, sel]
    logits = jnp.einsum('bqhd,bqkhd->bqhk', q, k_sel).astype(F32) * (HEAD_DIM ** -0.5)
    logits = jnp.where(valid[:, :, None, :], logits, -jnp.inf)
    p = jax.nn.softmax(logits, axis=-1).astype(v.dtype)
    return jnp.einsum('bqhk,bqkhd->bqhd', p, v_sel)


def moba_blocks(k, v):
    B, L, H, D = k.shape
    nb = -(-L // MOBA_BLOCK)
    pad = nb * MOBA_BLOCK - L
    kp = jnp.pad(k, ((0, 0), (0, pad), (0, 0), (0, 0)))
    vp = jnp.pad(v, ((0, 0), (0, pad), (0, 0), (0, 0)))
    kb = kp.reshape(B, nb, MOBA_BLOCK, H, D).transpose(0, 3, 1, 2, 4)
    vb = vp.reshape(B, nb, MOBA_BLOCK, H, D).transpose(0, 3, 1, 2, 4)
    k_mean = kb.astype(F32).mean(axis=3)
    return kb, vb, k_mean


def moba_attend(q, q_pos, kb, vb, k_mean):
    B, H, NB = k_mean.shape[0], k_mean.shape[1], k_mean.shape[2]
    Tq = q.shape[1]
    own = q_pos // MOBA_BLOCK
    gate = jnp.einsum('bqhd,bhnd->bqhn', q.astype(F32), k_mean)
    past = jnp.arange(NB, dtype=jnp.int32)[None, :] < own[:, None]
    gate = jnp.where(past[None, :, None, :], gate, -jnp.inf)
    kk = min(MOBA_TOPK, NB)
    _, sel = lax.top_k(gate, kk)
    own_b = jnp.broadcast_to(own[None, :, None, None], (B, Tq, H, 1))
    blocks = jnp.concatenate([sel, own_b], axis=-1)
    bvalid = jnp.concatenate([sel < own_b, jnp.ones_like(own_b, dtype=bool)], axis=-1)
    bi = jnp.arange(B)[:, None, None, None]
    hi = jnp.arange(H)[None, None, :, None]
    k_sel = kb[bi, hi, blocks]
    v_sel = vb[bi, hi, blocks]
    key_pos = blocks[..., None] * MOBA_BLOCK + jnp.arange(MOBA_BLOCK, dtype=jnp.int32)
    mask = bvalid[..., None] & (key_pos <= q_pos[None, :, None, None, None])
    logits = jnp.einsum('bqhd,bqhnkd->bqhnk', q, k_sel).astype(F32) * (HEAD_DIM ** -0.5)
    logits = jnp.where(mask, logits, -jnp.inf).reshape(B, Tq, H, -1)
    p = jax.nn.softmax(logits, axis=-1).astype(v_sel.dtype).reshape(mask.shape)
    return jnp.einsum('bqhnk,bqhnkd->bqhd', p, v_sel)


def fox_attend(q, k, v, cum_q, cum_k, q_pos):
    L = k.shape[1]
    logits = jnp.einsum('bqhd,bkhd->bhqk', q, k).astype(F32) * (HEAD_DIM ** -0.5)
    bias = jnp.transpose(cum_q, (0, 2, 1))[..., :, None] - jnp.transpose(cum_k, (0, 2, 1))[..., None, :]
    mask = jnp.arange(L, dtype=jnp.int32)[None, :] <= q_pos[:, None]
    logits = jnp.where(mask[None, None], logits + bias, -jnp.inf)
    p = jax.nn.softmax(logits, axis=-1).astype(v.dtype)
    return jnp.einsum('bhqk,bkhd->bqhd', p, v)


def ab_project(x, w_in, pos):
    B, T, _ = x.shape
    ha = N_HEADS_A * HEAD_DIM
    hb = N_HEADS_B * HEAD_DIM
    sizes = (ha, ha, ha, hb, hb, hb, IDX_HEADS * IDX_DIM, IDX_DIM, IDX_HEADS)
    h = jnp.einsum('btd,de->bte', x, w_in)
    q_a, k_a, v_a, q_b, k_b, v_b, q_i, k_i, w_i = jnp.split(h, np.cumsum(sizes)[:-1].tolist(), axis=-1)
    q_a = partial_rotary(q_a.reshape(B, T, N_HEADS_A, HEAD_DIM), pos)
    k_a = partial_rotary(k_a.reshape(B, T, N_HEADS_A, HEAD_DIM), pos)
    v_a = v_a.reshape(B, T, N_HEADS_A, HEAD_DIM)
    q_b = partial_rotary(q_b.reshape(B, T, N_HEADS_B, HEAD_DIM), pos)
    k_b = partial_rotary(k_b.reshape(B, T, N_HEADS_B, HEAD_DIM), pos)
    v_b = v_b.reshape(B, T, N_HEADS_B, HEAD_DIM)
    q_i = partial_rotary(q_i.reshape(B, T, IDX_HEADS, IDX_DIM), pos)
    k_i = partial_rotary(k_i.reshape(B, T, 1, IDX_DIM), pos)[:, :, 0]
    w_i = w_i * (IDX_HEADS ** -0.5)
    return q_a, k_a, v_a, q_b, k_b, v_b, q_i, k_i, w_i


def merge_ab(o_a, o_b, w_out):
    B, T = o_a.shape[0], o_a.shape[1]
    o = jnp.concatenate([o_a.reshape(B, T, -1), o_b.reshape(B, T, -1)], axis=-1)
    return jnp.einsum('bte,ed->btd', o, w_out)


def ab_prompt(x, w_in, w_out, pos):
    q_a, k_a, v_a, q_b, k_b, v_b, q_i, k_i, w_i = ab_project(x, w_in, pos)
    o_a = sweep_query_blocks(
        lambda qa_blk, qi_blk, wi_blk, p_blk: dsa_attend(qa_blk, k_a, v_a, qi_blk, k_i, wi_blk, p_blk),
        (q_a, q_i, w_i), pos)
    kb, vb, k_mean = moba_blocks(k_b, v_b)
    o_b = sweep_query_blocks(lambda qb_blk, p_blk: moba_attend(qb_blk, p_blk, kb, vb, k_mean), (q_b,), pos)
    return merge_ab(o_a, o_b, w_out), (k_a, v_a, k_i, k_b, v_b)


def ab_sample(x, w_in, w_out, pos, cache_a_k, cache_a_v, cache_a_idx, cache_b_k, cache_b_v, page_table):
    q_a, k_a, v_a, q_b, k_b, v_b, q_i, k_i, w_i = ab_project(x, w_in, pos)
    ka_f = jnp.concatenate([gather_pages(cache_a_k, page_table), k_a], axis=1)
    va_f = jnp.concatenate([gather_pages(cache_a_v, page_table), v_a], axis=1)
    ki_f = jnp.concatenate([gather_pages(cache_a_idx, page_table), k_i], axis=1)
    kb_f = jnp.concatenate([gather_pages(cache_b_k, page_table), k_b], axis=1)
    vb_f = jnp.concatenate([gather_pages(cache_b_v, page_table), v_b], axis=1)
    o_a = dsa_attend(q_a, ka_f, va_f, q_i, ki_f, w_i, pos)
    kb, vb, k_mean = moba_blocks(kb_f, vb_f)
    o_b = moba_attend(q_b, pos, kb, vb, k_mean)
    return merge_ab(o_a, o_b, w_out), (k_a, v_a, k_i, k_b, v_b)


def fox_project(x, w_in, b_forget):
    B, T, _ = x.shape
    hc = N_HEADS_C * HEAD_DIM
    h = jnp.einsum('btd,de->bte', x, w_in)
    q, k, v, f = jnp.split(h, [hc, 2 * hc, 3 * hc], axis=-1)
    q = q.reshape(B, T, N_HEADS_C, HEAD_DIM)
    k = k.reshape(B, T, N_HEADS_C, HEAD_DIM)
    v = v.reshape(B, T, N_HEADS_C, HEAD_DIM)
    logf = jax.nn.log_sigmoid((f + b_forget).astype(F32))
    return q, k, v, logf


def fox_prompt(x, w_in, b_forget, w_out, pos):
    B, T, _ = x.shape
    q, k, v, logf = fox_project(x, w_in, b_forget)
    cum = jnp.cumsum(logf, axis=1)
    o = sweep_query_blocks(lambda q_blk, cq_blk, p_blk: fox_attend(q_blk, k, v, cq_blk, cum, p_blk),
                           (q, cum), pos)
    y = jnp.einsum('bte,ed->btd', o.reshape(B, T, -1), w_out)
    return y, (k, v, logf.astype(x.dtype))


def fox_sample(x, w_in, b_forget, w_out, pos, cache_c_k, cache_c_v, cache_c_logf, page_table):
    B, T, _ = x.shape
    q, k, v, logf = fox_project(x, w_in, b_forget)
    k_f = jnp.concatenate([gather_pages(cache_c_k, page_table), k], axis=1)
    v_f = jnp.concatenate([gather_pages(cache_c_v, page_table), v], axis=1)
    logf_f = jnp.concatenate([gather_pages(cache_c_logf, page_table).astype(F32), logf], axis=1)
    cum = jnp.cumsum(logf_f, axis=1)
    o = fox_attend(q, k_f, v_f, cum[:, -T:], cum, pos)
    y = jnp.einsum('bte,ed->btd', o.reshape(B, T, -1), w_out)
    return y, (k, v, logf.astype(x.dtype))


def moe(x, router_w, router_bias, w_gate, w_up, w_down):
    B, T, D = x.shape
    xt = x.reshape(B * T, D)
    scores = jax.nn.sigmoid(jnp.einsum('nd,de->ne', xt, router_w).astype(F32))
    biased = scores + router_bias.astype(F32)
    grp = biased.reshape(-1, N_GROUPS, EXPERTS_PER_GROUP)
    grp_score = lax.top_k(grp, 2)[0].sum(-1)
    g_sel = jnp.argmax(grp_score, axis=-1)
    in_group = (jnp.arange(N_EXPERTS) // EXPERTS_PER_GROUP)[None, :] == g_sel[:, None]
    _, e_sel = lax.top_k(jnp.where(in_group, biased, -jnp.inf), TOPK_EXPERTS)
    w_sel = jnp.take_along_axis(scores, e_sel, axis=-1)
    w_sel = w_sel / w_sel.sum(-1, keepdims=True)
    gates = jnp.sum(jax.nn.one_hot(e_sel, N_EXPERTS, dtype=F32) * w_sel[..., None], axis=1)
    y = jnp.zeros_like(xt)
    for e in range(N_EXPERTS):
        h = jax.nn.silu(xt @ w_gate[e]) * (xt @ w_up[e])
        y = y + gates[:, e:e + 1].astype(x.dtype) * (h @ w_down[e])
    return y.reshape(B, T, D)


def setup_inputs(seed: int = 0) -> dict:
    key = jax.random.key(seed)
    ks = jax.random.split(key, 24)
    n_pages = PAST_LEN // PAGE_SIZE
    n_used = DEC_BATCH * n_pages
    n_phys = n_used + n_used // 4
    nrm = jax.random.normal
    ha = N_HEADS_A * HEAD_DIM
    hb = N_HEADS_B * HEAD_DIM
    hc = N_HEADS_C * HEAD_DIM

    x_prompt = nrm(ks[0], (BATCH, SEQ, D_MODEL), F32)
    x_sample = nrm(ks[1], (DEC_BATCH, DEC_SEQ, D_MODEL), F32)
    cache_a_k = nrm(ks[2], (n_phys, PAGE_SIZE, N_HEADS_A, HEAD_DIM), F32)
    cache_a_v = nrm(ks[3], (n_phys, PAGE_SIZE, N_HEADS_A, HEAD_DIM), F32) * BETA
    cache_a_idx = nrm(ks[4], (n_phys, PAGE_SIZE, IDX_DIM), F32)
    cache_b_k = nrm(ks[5], (n_phys, PAGE_SIZE, N_HEADS_B, HEAD_DIM), F32)
    cache_b_v = nrm(ks[6], (n_phys, PAGE_SIZE, N_HEADS_B, HEAD_DIM), F32) * BETA
    cache_c_k = nrm(ks[7], (n_phys, PAGE_SIZE, N_HEADS_C, HEAD_DIM), F32)
    cache_c_v = nrm(ks[8], (n_phys, PAGE_SIZE, N_HEADS_C, HEAD_DIM), F32) * BETA
    cache_c_logf = jax.nn.log_sigmoid(3.0 + nrm(ks[9], (n_phys, PAGE_SIZE, N_HEADS_C), F32))
    page_table = jax.random.permutation(ks[10], n_phys)[:n_used].reshape(DEC_BATCH, n_pages).astype(jnp.int32)

    ab_scale = jnp.concatenate([jnp.ones((2 * ha,), F32), jnp.full((ha,), BETA, F32),
                                jnp.ones((2 * hb,), F32), jnp.full((hb,), BETA, F32),
                                jnp.ones((IDX_HEADS * IDX_DIM + IDX_DIM + IDX_HEADS,), F32)])
    w_in_ab = nrm(ks[11], (D_MODEL, AB_WIDTH), F32) * (D_MODEL ** -0.5) * ab_scale
    w_out_ab = nrm(ks[12], (ha + hb, D_MODEL), F32) * ((ha + hb) ** -0.5) * BETA
    fox_scale = jnp.concatenate([jnp.ones((2 * hc,), F32), jnp.full((hc,), BETA, F32),
                                 jnp.ones((N_HEADS_C,), F32)])
    w_in_fox = nrm(ks[13], (D_MODEL, FOX_WIDTH), F32) * (D_MODEL ** -0.5) * fox_scale
    b_forget = jnp.linspace(1.0, 6.0, N_HEADS_C, dtype=F32) + 0.1 * nrm(ks[14], (N_HEADS_C,), F32)
    w_out_fox = nrm(ks[15], (hc, D_MODEL), F32) * (hc ** -0.5) * BETA
    ln_mix_g = 1.0 + 0.02 * nrm(ks[16], (DEPTH, D_MODEL), F32)
    ln_mix_b = 0.02 * nrm(ks[17], (DEPTH, D_MODEL), F32)
    ln_ffn_g = 1.0 + 0.02 * nrm(ks[18], (DEPTH, D_MODEL), F32)
    ln_ffn_b = 0.02 * nrm(ks[19], (DEPTH, D_MODEL), F32)
    router_w = nrm(ks[20], (D_MODEL, N_EXPERTS), F32) * (D_MODEL ** -0.5)
    router_bias = 0.01 * nrm(ks[21], (N_EXPERTS,), F32)
    ke = jax.random.split(ks[22], 3)
    exp_w_gate = nrm(ke[0], (DEPTH, N_EXPERTS, D_MODEL, D_EXPERT), F32) * (D_MODEL ** -0.5)
    exp_w_up = nrm(ke[1], (DEPTH, N_EXPERTS, D_MODEL, D_EXPERT), F32) * (D_MODEL ** -0.5)
    exp_w_down = nrm(ke[2], (DEPTH, N_EXPERTS, D_EXPERT, D_MODEL), F32) * (D_EXPERT ** -0.5) * BETA
    return {
        "x_prompt": x_prompt, "x_sample": x_sample,
        "cache_a_k": cache_a_k, "cache_a_v": cache_a_v, "cache_a_idx": cache_a_idx,
        "cache_b_k": cache_b_k, "cache_b_v": cache_b_v,
        "cache_c_k": cache_c_k, "cache_c_v": cache_c_v, "cache_c_logf": cache_c_logf,
        "page_table": page_table,
        "w_in_ab": w_in_ab, "w_out_ab": w_out_ab,
        "w_in_fox": w_in_fox, "b_forget": b_forget, "w_out_fox": w_out_fox,
        "ln_mix_g": ln_mix_g, "ln_mix_b": ln_mix_b, "ln_ffn_g": ln_ffn_g, "ln_ffn_b": ln_ffn_b,
        "router_w": router_w, "router_bias": router_bias,
        "exp_w_gate": exp_w_gate, "exp_w_up": exp_w_up, "exp_w_down": exp_w_down,
    }


def reference(x_prompt, x_sample, cache_a_k, cache_a_v, cache_a_idx, cache_b_k, cache_b_v,
              cache_c_k, cache_c_v, cache_c_logf, page_table,
              w_in_ab, w_out_ab, w_in_fox, b_forget, w_out_fox,
              ln_mix_g, ln_mix_b, ln_ffn_g, ln_ffn_b, router_w, router_bias,
              exp_w_gate, exp_w_up, exp_w_down):
    pos_p = jnp.arange(SEQ, dtype=jnp.int32)
    pos_s = PAST_LEN + jnp.arange(DEC_SEQ, dtype=jnp.int32)
    xp = x_prompt
    xs = x_sample
    for layer in range(DEPTH):
        if layer % 2 == 0:
            mp, (pa_k, pa_v, pa_idx, pb_k, pb_v) = ab_prompt(xp, w_in_ab, w_out_ab, pos_p)
            ms, (sa_k, sa_v, sa_idx, sb_k, sb_v) = ab_sample(
                xs, w_in_ab, w_out_ab, pos_s, cache_a_k, cache_a_v, cache_a_idx, cache_b_k, cache_b_v, page_table)
        else:
            mp, (pc_k, pc_v, pc_logf) = fox_prompt(xp, w_in_fox, b_forget, w_out_fox, pos_p)
            ms, (sc_k, sc_v, sc_logf) = fox_sample(
                xs, w_in_fox, b_forget, w_out_fox, pos_s, cache_c_k, cache_c_v, cache_c_logf, page_table)
        xp = layer_norm(ALPHA * xp + mp, ln_mix_g[layer], ln_mix_b[layer])
        xs = layer_norm(ALPHA * xs + ms, ln_mix_g[layer], ln_mix_b[layer])
        fp = moe(xp, router_w, router_bias, exp_w_gate[layer], exp_w_up[layer], exp_w_down[layer])
        fs = moe(xs, router_w, router_bias, exp_w_gate[layer], exp_w_up[layer], exp_w_down[layer])
        xp = layer_norm(ALPHA * xp + fp, ln_ffn_g[layer], ln_ffn_b[layer])
        xs = layer_norm(ALPHA * xs + fs, ln_ffn_g[layer], ln_ffn_b[layer])
    return (xp, xs, pa_k, pa_v, pa_idx, pb_k, pb_v, pc_k, pc_v, pc_logf,
            sa_k, sa_v, sa_idx, sb_k, sb_v, sc_k, sc_v, sc_logf)
```

```python
import functools

import jax
import jax.numpy as jnp
from jax import lax
from jax.experimental import pallas as pl
from jax.experimental.pallas import tpu as pltpu

F32 = jnp.float32
BF16 = jnp.bfloat16
I32 = jnp.int32

D_MODEL = 1024
DEPTH = 2
PAGE_SIZE = 128
HEAD_DIM = 64
N_HEADS_A = 8
N_HEADS_B = 8
N_HEADS_C = 16
IDX_HEADS = 8
IDX_DIM = 64
DSA_TOPK = 256
MOBA_BLOCK = 256
MOBA_TOPK = 3
ROPE_THETA = 500000.0
N_EXPERTS = 16
N_GROUPS = 4
EXPERTS_PER_GROUP = N_EXPERTS // N_GROUPS
D_EXPERT = 512
ALPHA = (2 * DEPTH) ** 0.25
LN_EPS = 1e-5
HA = N_HEADS_A * HEAD_DIM
HB = N_HEADS_B * HEAD_DIM
HC = N_HEADS_C * HEAD_DIM
QK_SCALE = HEAD_DIM ** -0.5
IDX_SCALE = IDX_DIM ** -0.5

LANES = 128
NEG = -1e30
INT_MIN = -2 ** 31
VMEM_LIMIT = 56 * 2 ** 20

PROJ_TM = 256
DSA_TQ = 128
DSA_TK = 512
MOBA_T = MOBA_BLOCK
FOX_T = 256
MOE_TM = 512


def _cparams(*sem):
    return pltpu.CompilerParams(dimension_semantics=sem, vmem_limit_bytes=VMEM_LIMIT)


def _dot(a, b):
    return jnp.dot(a, b, preferred_element_type=F32)


def _dot_nt(a, b):
    return lax.dot_general(a, b, (((1,), (1,)), ((), ())), preferred_element_type=F32)


def _split2(x):
    hi = x.astype(BF16)
    lo = (x - hi.astype(F32)).astype(BF16)
    return hi, lo


def _split3(x):
    hi = x.astype(BF16)
    r = x - hi.astype(F32)
    mid = r.astype(BF16)
    lo = (r - mid.astype(F32)).astype(BF16)
    return hi, mid, lo


def _dot3(a_hi, a_lo, b_hi, b_lo, dot):
    return dot(a_hi, b_hi) + (dot(a_hi, b_lo) + dot(a_lo, b_hi))


def _layer_norm(z, g, b):
    mu = jnp.mean(z, axis=-1, keepdims=True)
    d = z - mu
    var = jnp.mean(d * d, axis=-1, keepdims=True)
    return d * lax.rsqrt(var + LN_EPS) * g + b


def _log_sigmoid(z):
    return -(jnp.maximum(-z, 0.0) + jnp.log1p(jnp.exp(-jnp.abs(z))))


def _rotary_tables(pos_ref, invf_ref):
    ang = pos_ref[...] * invf_ref[...]
    c = jnp.cos(ang)
    s = jnp.sin(ang)
    f = lax.broadcasted_iota(I32, ang.shape, 1) % HEAD_DIM
    s_up = jnp.where(f < 8, -s, 0.0)
    s_dn = jnp.where(f >= 8, s, 0.0)
    return c, s_up, s_dn


def _rotate(h, c, s_up, s_dn):
    outs = []
    for j in range(h.shape[1] // LANES):
        hc = h[:, j * LANES:(j + 1) * LANES]
        outs.append(hc * c + pltpu.roll(hc, LANES - 8, 1) * s_up + pltpu.roll(hc, 8, 1) * s_dn)
    return outs[0] if len(outs) == 1 else jnp.concatenate(outs, axis=1)


def _ab_proj_kernel(x_ref, w_ref, pos_ref, invf_ref,
                    qa_ref, ka_ref, va_ref, qb_ref, kb_ref, vb_ref, qi_ref, tail_ref,
                    ka16_ref, va16_ref, kb16_ref, vb16_ref):
    xb = x_ref[...].astype(BF16)
    c, s_up, s_dn = _rotary_tables(pos_ref, invf_ref)

    def seg(j, width=HA):
        return _dot(xb, w_ref[:, j * HA:j * HA + width])

    qa_ref[...] = _rotate(seg(0), c, s_up, s_dn)
    ka = _rotate(seg(1), c, s_up, s_dn)
    ka_ref[...] = ka
    ka16_ref[...] = ka.astype(BF16)
    va = seg(2)
    va_ref[...] = va
    va16_ref[...] = va.astype(BF16)
    qb_ref[...] = _rotate(seg(3), c, s_up, s_dn)
    kb = _rotate(seg(4), c, s_up, s_dn)
    kb_ref[...] = kb
    kb16_ref[...] = kb.astype(BF16)
    vb = seg(5)
    vb_ref[...] = vb
    vb16_ref[...] = vb.astype(BF16)
    qi_ref[...] = _rotate(seg(6), c, s_up, s_dn)
    t = seg(7, LANES)
    lane = lax.broadcasted_iota(I32, t.shape, 1)
    is_key = lane < IDX_DIM
    ct = jnp.where(is_key, c, IDX_HEADS ** -0.5)
    tail_ref[...] = (t * ct + pltpu.roll(t, LANES - 8, 1) * jnp.where(is_key, s_up, 0.0)
                     + pltpu.roll(t, 8, 1) * jnp.where(is_key, s_dn, 0.0))


def _ab_project(x2d, w16, pos, invf, tm):
    n = x2d.shape[0]
    wide = jax.ShapeDtypeStruct((n, HA), F32)
    wide16 = jax.ShapeDtypeStruct((n, HA), BF16)
    row = lambda w: pl.BlockSpec((tm, w), lambda i: (i, 0))
    full = lambda a: pl.BlockSpec(a.shape, lambda i: (0, 0))
    return pl.pallas_call(
        _ab_proj_kernel,
        grid=(n // tm,),
        in_specs=[row(D_MODEL), full(w16), row(1), full(invf)],
        out_specs=[row(HA)] * 7 + [row(LANES)] + [row(HA)] * 4,
        out_shape=[wide] * 7 + [jax.ShapeDtypeStruct((n, LANES), F32)] + [wide16] * 4,
        compiler_params=_cparams("parallel"),
        name="ab_project",
    )(x2d, w16, pos, invf)


def _fox_proj_kernel(x_ref, w_ref, bf_ref, q_ref, k_ref, v_ref, logf_ref, cum_ref,
                     k16_ref, v16_ref, carry_ref, *, tiles_per_seq):
    i = pl.program_id(0)
    xb = x_ref[...].astype(BF16)
    q_ref[...] = _dot(xb, w_ref[:, 0:HC])
    k = _dot(xb, w_ref[:, HC:2 * HC])
    k_ref[...] = k
    k16_ref[...] = k.astype(BF16)
    v = _dot(xb, w_ref[:, 2 * HC:3 * HC])
    v_ref[...] = v
    v16_ref[...] = v.astype(BF16)
    f = _dot(xb, w_ref[:, 3 * HC:3 * HC + LANES])
    logf = _log_sigmoid(f + bf_ref[...])
    logf_ref[...] = logf

    @pl.when(i % tiles_per_seq == 0)
    def _():
        carry_ref[...] = jnp.zeros_like(carry_ref)

    tm = logf.shape[0]
    r = lax.broadcasted_iota(I32, (tm, tm), 0)
    cc = lax.broadcasted_iota(I32, (tm, tm), 1)
    tril = jnp.where(cc <= r, 1.0, 0.0).astype(BF16)
    hi, mid, lo = _split3(logf)
    cum = (_dot(tril, hi) + _dot(tril, mid) + _dot(tril, lo)) + carry_ref[...]
    cum_ref[...] = cum
    carry_ref[...] = cum[tm - 1:tm, :]


def _fox_project(x2d, w16, bf_pad, tm, rows_per_seq):
    n = x2d.shape[0]
    wide = jax.ShapeDtypeStruct((n, HC), F32)
    wide16 = jax.ShapeDtypeStruct((n, HC), BF16)
    small = jax.ShapeDtypeStruct((n, LANES), F32)
    row = lambda w: pl.BlockSpec((tm, w), lambda i: (i, 0))
    full = lambda a: pl.BlockSpec(a.shape, lambda i: (0, 0))
    return pl.pallas_call(
        functools.partial(_fox_proj_kernel, tiles_per_seq=rows_per_seq // tm),
        grid=(n // tm,),
        in_specs=[row(D_MODEL), full(w16), full(bf_pad)],
        out_specs=[row(HC)] * 3 + [row(LANES)] * 2 + [row(HC)] * 2,
        out_shape=[wide] * 3 + [small] * 2 + [wide16] * 2,
        scratch_shapes=[pltpu.VMEM((1, LANES), F32)],
        compiler_params=_cparams("arbitrary"),
        name="fox_project",
    )(x2d, w16, bf_pad)


def _out_ln_kernel(o_ref, w_ref, x_ref, g_ref, b_ref, y_ref):
    m = _dot(o_ref[...], w_ref[...])
    y_ref[...] = _layer_norm(ALPHA * x_ref[...] + m, g_ref[...], b_ref[...])


def _out_proj_ln(o16, w16, x2d, g, b, tm):
    n, k = o16.shape
    row = lambda w: pl.BlockSpec((tm, w), lambda i: (i, 0))
    full = lambda a: pl.BlockSpec(a.shape, lambda i: (0, 0))
    return pl.pallas_call(
        _out_ln_kernel,
        grid=(n // tm,),
        in_specs=[row(k), full(w16), row(D_MODEL), full(g), full(b)],
        out_specs=row(D_MODEL),
        out_shape=jax.ShapeDtypeStruct((n, D_MODEL), F32),
        compiler_params=_cparams("parallel"),
        name="out_proj_ln",
    )(o16, w16, x2d, g, b)


def _softmax_init(m_scr, l_scr, acc_scr):
    m_scr[...] = jnp.full(m_scr.shape, NEG, F32)
    l_scr[...] = jnp.zeros(l_scr.shape, F32)
    acc_scr[...] = jnp.zeros(acc_scr.shape, F32)


def _softmax_step(h, s, v, m_scr, l_scr, acc_scr):
    m_old = m_scr[h]
    m_new = jnp.maximum(m_old, jnp.max(s, axis=1, keepdims=True))
    alpha = jnp.exp(m_old - m_new)
    p = jnp.exp(s - m_new)
    l_scr[h] = alpha * l_scr[h] + jnp.sum(p, axis=1, keepdims=True)
    acc_scr[h] = alpha * acc_scr[h] + _dot(p.astype(BF16), v)
    m_scr[h] = m_new


def _pair_masks(shape):
    lane = lax.broadcasted_iota(I32, shape, 1)
    return lane < HEAD_DIM


def _store_pair_queries(q, qm_scr, base):
    lo_head = _pair_masks(q.shape)
    qs = (q * QK_SCALE).astype(BF16)
    zero = jnp.zeros_like(qs)
    qm_scr[base] = jnp.where(lo_head, qs, zero)
    qm_scr[base + 1] = jnp.where(lo_head, zero, qs)


def _pair_output(hp, l_scr, acc_scr):
    a0 = acc_scr[2 * hp] / l_scr[2 * hp]
    a1 = acc_scr[2 * hp + 1] / l_scr[2 * hp + 1]
    return jnp.where(_pair_masks(a0.shape), a0, a1)


def _sortable_key(x):
    b = pltpu.bitcast(x, I32)
    return jnp.where(b < 0, INT_MIN - b, b)


def _count_rows(key_scr, n_tiles, pred):
    _, rows, tk = key_scr.shape

    def body(j, acc):
        kt = key_scr[j]
        for cidx in range(tk // LANES):
            chunk = kt[:, cidx * LANES:(cidx + 1) * LANES]
            acc = acc + jnp.where(pred(chunk, j * tk + cidx * LANES), 1, 0)
        return acc

    acc = lax.fori_loop(0, n_tiles, body, jnp.zeros((rows, LANES), I32))
    return jnp.sum(acc, axis=1, keepdims=True)


def _topk_threshold(key_scr, n_tiles, topk, col_bits):
    _, rows, _ = key_scr.shape

    def bit_body(i, t):
        cand = t + jnp.left_shift(jnp.int32(1), 31 - i)
        cnt = _count_rows(key_scr, n_tiles, lambda k, c0: k >= cand)
        return jnp.where(cnt >= topk, cand, t)

    t = lax.fori_loop(0, 32, bit_body, jnp.full((rows, 1), INT_MIN, I32))
    n_gt = _count_rows(key_scr, n_tiles, lambda k, c0: k > t)
    n_eq = _count_rows(key_scr, n_tiles, lambda k, c0: k == t)
    need = topk - n_gt
    excess = jnp.max(jnp.where(n_eq > need, 1, 0)) > 0

    def tie_search():
        def body(i, x):
            cand = x + jnp.left_shift(jnp.int32(1), col_bits - 1 - i)

            def pred(k, c0):
                col = c0 + lax.broadcasted_iota(I32, k.shape, 1)
                return (k == t) & (col < cand)

            cnt = _count_rows(key_scr, n_tiles, pred)
            return jnp.where(cnt < need, cand, x)

        return lax.fori_loop(0, col_bits, body, jnp.zeros((rows, 1), I32))

    c = lax.cond(excess, tie_search, lambda: jnp.full((rows, 1), 2 ** 31 - 1, I32))
    c = jnp.where(t == INT_MIN, -1, c)
    return t, c


def _dsa_prompt_kernel(qa_ref, qi_ref, tail_ref, kith_ref, kitl_ref, k_ref, v_ref, o_ref,
                       key_scr, qih_scr, qil_scr, qm_scr, m_scr, l_scr, acc_scr,
                       *, tq, tk, topk, col_bits):
    qt = pl.program_id(1)
    n_tiles = ((qt + 1) * tq + tk - 1) // tk
    row = qt * tq + lax.broadcasted_iota(I32, (tq, 1), 0)

    qi = qi_ref[...]
    for h in range(IDX_HEADS):
        hi, lo = _split2(qi[:, h * IDX_DIM:(h + 1) * IDX_DIM])
        qih_scr[h] = hi
        qil_scr[h] = lo
    tail = tail_ref[...]

    def score_tile(j, diagonal):
        kh = kith_ref[j]
        kl = kitl_ref[j]
        sc = jnp.zeros((tq, tk), F32)
        for h in range(IDX_HEADS):
            d = _dot3(qih_scr[h], qil_scr[h], kh, kl, _dot)
            sc = sc + tail[:, IDX_DIM + h:IDX_DIM + h + 1] * jnp.maximum(d, 0.0)
        key = _sortable_key(sc * IDX_SCALE)
        if diagonal:
            col = j * tk + lax.broadcasted_iota(I32, (tq, tk), 1)
            key = jnp.where(col <= row, key, INT_MIN)
        key_scr[j] = key

    lax.fori_loop(0, n_tiles - 1, lambda j, _: (score_tile(j, False), 0)[1], 0)
    score_tile(n_tiles - 1, True)

    t, c = _topk_threshold(key_scr, n_tiles, topk, col_bits)

    qa = qa_ref[...]
    for hp in range(N_HEADS_A // 2):
        _store_pair_queries(qa[:, hp * LANES:(hp + 1) * LANES], qm_scr, 2 * hp)
    _softmax_init(m_scr, l_scr, acc_scr)

    def attn_tile(j, _):
        kt = key_scr[j]
        col = j * tk + lax.broadcasted_iota(I32, (tq, tk), 1)
        msk = (kt > t) | ((kt == t) & (col <= c))
        for hp in range(N_HEADS_A // 2):
            k = k_ref[pl.ds(j * tk, tk), hp * LANES:(hp + 1) * LANES]
            v = v_ref[pl.ds(j * tk, tk), hp * LANES:(hp + 1) * LANES]
            for h in (2 * hp, 2 * hp + 1):
                s = jnp.where(msk, _dot_nt(qm_scr[h], k), NEG)
                _softmax_step(h, s, v, m_scr, l_scr, acc_scr)
        return 0

    lax.fori_loop(0, n_tiles, attn_tile, 0)
    for hp in range(N_HEADS_A // 2):
        o_ref[:, hp * LANES:(hp + 1) * LANES] = _pair_output(hp, l_scr, acc_scr).astype(BF16)


def _dsa_prompt(qa, qi, tail, kit_hi, kit_lo, k16, v16, batch, seq, tq, tk):
    nq = seq // tq
    nk = seq // tk
    topk = min(DSA_TOPK, seq // 4)
    col_bits = max(1, (seq - 1).bit_length())
    qrow = lambda w: pl.BlockSpec((tq, w), lambda b, i: (b * nq + i, 0))
    kit = pl.BlockSpec((None, nk, IDX_DIM, tk), lambda b, i: (b, 0, 0, 0))
    kv = pl.BlockSpec((seq, HA), lambda b, i: (b, 0))
    return pl.pallas_call(
        functools.partial(_dsa_prompt_kernel, tq=tq, tk=tk, topk=topk, col_bits=col_bits),
        grid=(batch, nq),
        in_specs=[qrow(HA), qrow(HA), qrow(LANES), kit, kit, kv, kv],
        out_specs=qrow(HA),
        out_shape=jax.ShapeDtypeStruct((batch * seq, HA), BF16),
        scratch_shapes=[
            pltpu.VMEM((nk, tq, tk), I32),
            pltpu.VMEM((IDX_HEADS, tq, IDX_DIM), BF16),
            pltpu.VMEM((IDX_HEADS, tq, IDX_DIM), BF16),
            pltpu.VMEM((N_HEADS_A, tq, LANES), BF16),
            pltpu.VMEM((N_HEADS_A, tq, 1), F32),
            pltpu.VMEM((N_HEADS_A, tq, 1), F32),
            pltpu.VMEM((N_HEADS_A, tq, LANES), F32),
        ],
        compiler_params=_cparams("parallel", "parallel"),
        name="dsa_prompt",
    )(qa, qi, tail, kit_hi, kit_lo, k16, v16)


def _block_mean_kernel(k_ref, o_ref):
    o_ref[...] = jnp.mean(k_ref[...], axis=0, keepdims=True)[None]


def _block_means(k2d):
    n = k2d.shape[0]
    nb = n // MOBA_BLOCK
    return pl.pallas_call(
        _block_mean_kernel,
        grid=(nb,),
        in_specs=[pl.BlockSpec((MOBA_BLOCK, HB), lambda i: (i, 0))],
        out_specs=pl.BlockSpec((1, 1, HB), lambda i: (i, 0, 0)),
        out_shape=jax.ShapeDtypeStruct((nb, 1, HB), F32),
        compiler_params=_cparams("parallel"),
        name="moba_block_means",
    )(k2d)


def _moba_prompt_kernel(q_ref, km_ref, k_ref, v_ref, o_ref,
                        sel_scr, qm_scr, m_scr, l_scr, acc_scr, *, t):
    qt = pl.program_id(2)
    q = q_ref[...]
    nb = km_ref.shape[0]
    lo_head = _pair_masks(q.shape)
    km_hi, km_lo = _split2(km_ref[...])
    blk = lax.broadcasted_iota(I32, (t, nb), 1)
    for hh in range(2):
        qh = jnp.where(lo_head if hh == 0 else ~lo_head, q, 0.0)
        q_hi, q_lo = _split2(qh)
        g = _dot3(q_hi, q_lo, km_hi, km_lo, _dot_nt)
        g = jnp.where(blk < qt, g, -jnp.inf)
        sel = jnp.zeros((t, nb), F32)
        for _ in range(MOBA_TOPK):
            mx = jnp.max(g, axis=1, keepdims=True)
            first = jnp.min(jnp.where(g == mx, blk, nb), axis=1, keepdims=True)
            pick = (blk == first) & (mx > -jnp.inf)
            sel = jnp.where(pick, 1.0, sel)
            g = jnp.where(blk == first, -jnp.inf, g)
        sel_scr[hh] = sel
    _store_pair_queries(q, qm_scr, 0)
    _softmax_init(m_scr, l_scr, acc_scr)

    def past_tile(j, _):
        k = k_ref[pl.ds(j * t, t), :]
        v = v_ref[pl.ds(j * t, t), :]
        for hh in range(2):
            chosen = jnp.max(jnp.where(blk == j, sel_scr[hh], 0.0), axis=1, keepdims=True) > 0.0
            s = jnp.where(chosen, _dot_nt(qm_scr[hh], k), NEG)
            _softmax_step(hh, s, v, m_scr, l_scr, acc_scr)
        return 0

    lax.fori_loop(0, qt, past_tile, 0)
    k = k_ref[pl.ds(qt * t, t), :]
    v = v_ref[pl.ds(qt * t, t), :]
    causal = lax.broadcasted_iota(I32, (t, t), 1) <= lax.broadcasted_iota(I32, (t, t), 0)
    for hh in range(2):
        s = jnp.where(causal, _dot_nt(qm_scr[hh], k), NEG)
        _softmax_step(hh, s, v, m_scr, l_scr, acc_scr)
    o_ref[...] = _pair_output(0, l_scr, acc_scr).astype(BF16)


def _moba_prompt(q, kmean, k16, v16, batch, seq):
    t = MOBA_T
    nq = seq // t
    nb = seq // MOBA_BLOCK
    npair = N_HEADS_B // 2
    return pl.pallas_call(
        functools.partial(_moba_prompt_kernel, t=t),
        grid=(batch, npair, nq),
        in_specs=[
            pl.BlockSpec((t, LANES), lambda b, p, i: (b * nq + i, p)),
            pl.BlockSpec((None, nb, LANES), lambda b, p, i: (b, 0, p)),
            pl.BlockSpec((seq, LANES), lambda b, p, i: (b, p)),
            pl.BlockSpec((seq, LANES), lambda b, p, i: (b, p)),
        ],
        out_specs=pl.BlockSpec((t, LANES), lambda b, p, i: (b * nq + i, p)),
        out_shape=jax.ShapeDtypeStruct((batch * seq, HB), BF16),
        scratch_shapes=[
            pltpu.VMEM((2, t, nb), F32),
            pltpu.VMEM((2, t, LANES), BF16),
            pltpu.VMEM((2, t, 1), F32),
            pltpu.VMEM((2, t, 1), F32),
            pltpu.VMEM((2, t, LANES), F32),
        ],
        compiler_params=_cparams("parallel", "parallel", "parallel"),
        name="moba_prompt",
    )(q, kmean, k16, v16)


def _fox_prompt_kernel(q_ref, cq_ref, ck_ref, k_ref, v_ref, o_ref,
                       qm_scr, m_scr, l_scr, acc_scr, *, t):
    qt = pl.program_id(2)
    _store_pair_queries(q_ref[...], qm_scr, 0)
    _softmax_init(m_scr, l_scr, acc_scr)
    cq = cq_ref[...]

    def tile(j, causal):
        k = k_ref[pl.ds(j * t, t), :]
        v = v_ref[pl.ds(j * t, t), :]
        ck = ck_ref[j]
        for hh in range(2):
            s = _dot_nt(qm_scr[hh], k) + (cq[:, hh:hh + 1] - ck[hh:hh + 1, :])
            if causal is not None:
                s = jnp.where(causal, s, NEG)
            _softmax_step(hh, s, v, m_scr, l_scr, acc_scr)

    lax.fori_loop(0, qt, lambda j, _: (tile(j, None), 0)[1], 0)
    tile(qt, lax.broadcasted_iota(I32, (t, t), 1) <= lax.broadcasted_iota(I32, (t, t), 0))
    o_ref[...] = _pair_output(0, l_scr, acc_scr).astype(BF16)


def _fox_prompt(q, cum_q, cum_k, k16, v16, batch, seq, t):
    nq = seq // t
    npair = N_HEADS_C // 2
    return pl.pallas_call(
        functools.partial(_fox_prompt_kernel, t=t),
        grid=(batch, npair, nq),
        in_specs=[
            pl.BlockSpec((t, LANES), lambda b, p, i: (b * nq + i, p)),
            pl.BlockSpec((None, None, t, 2), lambda b, p, i: (b, p, i, 0)),
            pl.BlockSpec((None, None, nq, 2, t), lambda b, p, i: (b, p, 0, 0, 0)),
            pl.BlockSpec((seq, LANES), lambda b, p, i: (b, p)),
            pl.BlockSpec((seq, LANES), lambda b, p, i: (b, p)),
        ],
        out_specs=pl.BlockSpec((t, LANES), lambda b, p, i: (b * nq + i, p)),
        out_shape=jax.ShapeDtypeStruct((batch * seq, HC), BF16),
        scratch_shapes=[
            pltpu.VMEM((2, t, LANES), BF16),
            pltpu.VMEM((2, t, 1), F32),
            pltpu.VMEM((2, t, 1), F32),
            pltpu.VMEM((2, t, LANES), F32),
        ],
        compiler_params=_cparams("parallel", "parallel", "parallel"),
        name="fox_prompt",
    )(q, cum_q, cum_k, k16, v16)


def _router_kernel(x_ref, rwh_ref, rwl_ref, rb_ref, g_ref):
    x_hi, x_lo = _split2(x_ref[...])
    logits = _dot3(rwh_ref[...], rwl_ref[...], x_hi, x_lo, _dot_nt)
    scores = 1.0 / (1.0 + jnp.exp(-logits))
    biased = scores + rb_ref[...]
    rows = [biased[e:e + 1, :] for e in range(N_EXPERTS)]
    ninf = jnp.full_like(rows[0], -jnp.inf)

    def top2(vals):
        mx = functools.reduce(jnp.maximum, vals)
        picks1, found = [], jnp.zeros_like(mx) > 0
        for vv in vals:
            p = (vv == mx) & ~found
            found = found | p
            picks1.append(p)
        rest = [jnp.where(p, ninf, vv) for p, vv in zip(picks1, vals)]
        mx2 = functools.reduce(jnp.maximum, rest)
        picks2, found = [], jnp.zeros_like(mx) > 0
        for vv in rest:
            p = (vv == mx2) & ~found
            found = found | p
            picks2.append(p)
        return mx, mx2, picks1, picks2

    grp_score = []
    for gi in range(N_GROUPS):
        m1, m2, _, _ = top2(rows[gi * EXPERTS_PER_GROUP:(gi + 1) * EXPERTS_PER_GROUP])
        grp_score.append(m1 + m2)
    best = grp_score[0]
    g_sel = jnp.zeros_like(best, dtype=I32)
    for gi in range(1, N_GROUPS):
        better = grp_score[gi] > best
        best = jnp.where(better, grp_score[gi], best)
        g_sel = jnp.where(better, gi, g_sel)
    masked = [jnp.where(g_sel == e // EXPERTS_PER_GROUP, rows[e], ninf) for e in range(N_EXPERTS)]
    _, _, p1, p2 = top2(masked)
    zero = jnp.zeros_like(best)
    w1 = functools.reduce(jnp.add, [jnp.where(p1[e], scores[e:e + 1, :], zero) for e in range(N_EXPERTS)])
    w2 = functools.reduce(jnp.add, [jnp.where(p2[e], scores[e:e + 1, :], zero) for e in range(N_EXPERTS)])
    tot = w1 + w2
    for e in range(N_EXPERTS):
        g_ref[e:e + 1, :] = jnp.where(p1[e], w1 / tot, zero) + jnp.where(p2[e], w2 / tot, zero)


def _router(x2d, rwt_hi, rwt_lo, rb, tm):
    n = x2d.shape[0]
    full = lambda a: pl.BlockSpec(a.shape, lambda i: (0, 0))
    return pl.pallas_call(
        _router_kernel,
        grid=(n // tm,),
        in_specs=[pl.BlockSpec((tm, D_MODEL), lambda i: (i, 0)), full(rwt_hi), full(rwt_lo), full(rb)],
        out_specs=pl.BlockSpec((N_EXPERTS, tm), lambda i: (0, i)),
        out_shape=jax.ShapeDtypeStruct((N_EXPERTS, n), F32),
        compiler_params=_cparams("parallel"),
        name="moe_router",
    )(x2d, rwt_hi, rwt_lo, rb)


def _moe_kernel(x_ref, gates_ref, wg_ref, wu_ref, wd_ref, g_ref, b_ref, y_ref, xb_scr, acc_scr):
    e = pl.program_id(1)

    @pl.when(e == 0)
    def _():
        xb_scr[...] = x_ref[...].astype(BF16)
        acc_scr[...] = jnp.zeros_like(acc_scr)

    xb = xb_scr[...]
    gate = _dot(xb, wg_ref[0])
    up = _dot(xb, wu_ref[0])
    h = (gate * (1.0 / (1.0 + jnp.exp(-gate))) * up).astype(BF16)
    down = _dot(h, wd_ref[0])
    gates = gates_ref[...]
    lane = lax.broadcasted_iota(I32, gates.shape, 1)
    w = jnp.sum(jnp.where(lane == e, gates, 0.0), axis=1, keepdims=True)
    acc_scr[...] += w * down

    @pl.when(e == pl.num_programs(1) - 1)
    def _():
        y_ref[...] = _layer_norm(ALPHA * x_ref[...] + acc_scr[...], g_ref[...], b_ref[...])


def _moe_ln(x2d, gates, wg16, wu16, wd16, g, b, tm):
    n = x2d.shape[0]
    full = lambda a: pl.BlockSpec(a.shape, lambda i, e: (0, 0))
    return pl.pallas_call(
        _moe_kernel,
        grid=(n // tm, N_EXPERTS),
        in_specs=[
            pl.BlockSpec((tm, D_MODEL), lambda i, e: (i, 0)),
            pl.BlockSpec((tm, N_EXPERTS), lambda i, e: (i, 0)),
            pl.BlockSpec((1, D_MODEL, D_EXPERT), lambda i, e: (e, 0, 0)),
            pl.BlockSpec((1, D_MODEL, D_EXPERT), lambda i, e: (e, 0, 0)),
            pl.BlockSpec((1, D_EXPERT, D_MODEL), lambda i, e: (e, 0, 0)),
            full(g), full(b),
        ],
        out_specs=pl.BlockSpec((tm, D_MODEL), lambda i, e: (i, 0)),
        out_shape=jax.ShapeDtypeStruct((n, D_MODEL), F32),
        scratch_shapes=[pltpu.VMEM((tm, D_MODEL), BF16), pltpu.VMEM((tm, D_MODEL), F32)],
        compiler_params=_cparams("parallel", "arbitrary"),
        name="moe_experts_ln",
    )(x2d, gates, wg16, wu16, wd16, g, b)


def _head_rows(row, n_heads):
    width = row.shape[1]
    r = lax.broadcasted_iota(I32, (n_heads, width), 0)
    lane = lax.broadcasted_iota(I32, (n_heads, width), 1)
    return jnp.where(lane // HEAD_DIM == r, jnp.broadcast_to(row, (n_heads, width)), 0.0)


def _fold_heads(x):
    n_heads, width = x.shape
    r = lax.broadcasted_iota(I32, x.shape, 0)
    lane = lax.broadcasted_iota(I32, x.shape, 1)
    return jnp.sum(jnp.where(lane // HEAD_DIM == r, x, 0.0), axis=0, keepdims=True)


def _dsa_sample_score_kernel(pt_ref, q_ref, w_ref, knew_ref, *rest, n_pages):
    page_refs = rest[:n_pages]
    s_ref = rest[n_pages]
    q = q_ref[...]
    q_hi, q_lo = _split2(q)
    w = w_ref[...]
    for p in range(n_pages):
        k_hi, k_lo = _split2(page_refs[p][...])
        d = _dot3(q_hi, q_lo, k_hi, k_lo, _dot_nt)
        s_ref[p:p + 1, :] = jnp.sum(w * jnp.maximum(d, 0.0), axis=0, keepdims=True) * IDX_SCALE
    k_new = knew_ref[...][:, 0:IDX_DIM]
    d_new = jnp.sum(q * k_new, axis=1, keepdims=True)
    s_new = jnp.sum(w * jnp.maximum(d_new, 0.0), axis=0, keepdims=True) * IDX_SCALE
    lane = lax.broadcasted_iota(I32, (1, PAGE_SIZE), 1)
    s_ref[n_pages:n_pages + 1, :] = jnp.where(lane == 0, s_new, -jnp.inf)


def _dsa_sample_scores(page_table, q_i, w_i, tail, cache_idx):
    nseq, n_pages = page_table.shape
    page = lambda p: pl.BlockSpec((None, PAGE_SIZE, IDX_DIM), lambda b, pt: (pt[b, p], 0, 0))
    grid_spec = pltpu.PrefetchScalarGridSpec(
        num_scalar_prefetch=1,
        grid=(nseq,),
        in_specs=[
            pl.BlockSpec((None, IDX_HEADS, IDX_DIM), lambda b, pt: (b, 0, 0)),
            pl.BlockSpec((None, IDX_HEADS, 1), lambda b, pt: (b, 0, 0)),
            pl.BlockSpec((None, 1, LANES), lambda b, pt: (b, 0, 0)),
        ] + [page(p) for p in range(n_pages)],
        out_specs=pl.BlockSpec((None, n_pages + 1, PAGE_SIZE), lambda b, pt: (b, 0, 0)),
    )
    return pl.pallas_call(
        functools.partial(_dsa_sample_score_kernel, n_pages=n_pages),
        grid_spec=grid_spec,
        out_shape=jax.ShapeDtypeStruct((nseq, n_pages + 1, PAGE_SIZE), F32),
        compiler_params=_cparams("parallel"),
        name="dsa_sample_scores",
    )(page_table, q_i, w_i, tail, *([cache_idx] * n_pages))


def _select_kernel(s_ref, valid_ref, bias_ref, key_scr, *, topk, col_bits):
    n_tiles = s_ref.shape[0]
    for j in range(n_tiles):
        key_scr[j] = jnp.where(valid_ref[j] > 0.0, _sortable_key(s_ref[j]), INT_MIN)
    t, c = _topk_threshold(key_scr, n_tiles, topk, col_bits)
    for j in range(n_tiles):
        kt = key_scr[j]
        col = j * PAGE_SIZE + lax.broadcasted_iota(I32, kt.shape, 1)
        bias_ref[j] = jnp.where((kt > t) | ((kt == t) & (col <= c)), 0.0, NEG)


def _select_topk_bias(scores_t, valid_t, topk):
    n_chunks, nseq, _ = scores_t.shape
    col_bits = max(1, (n_chunks * PAGE_SIZE - 1).bit_length())
    return pl.pallas_call(
        functools.partial(_select_kernel, topk=topk, col_bits=col_bits),
        out_shape=jax.ShapeDtypeStruct(scores_t.shape, F32),
        scratch_shapes=[pltpu.VMEM(scores_t.shape, I32)],
        compiler_params=pltpu.CompilerParams(vmem_limit_bytes=VMEM_LIMIT),
        name="dsa_sample_select",
    )(scores_t, valid_t)


def _decode_step(s, v, m_scr, l_scr, acc_scr):
    m_old = m_scr[...]
    m_new = jnp.maximum(m_old, jnp.max(s, axis=1, keepdims=True))
    alpha = jnp.exp(m_old - m_new)
    p = jnp.exp(s - m_new)
    l_scr[...] = alpha * l_scr[...] + jnp.sum(p, axis=1, keepdims=True)
    acc_scr[...] = alpha * acc_scr[...] + _dot(p.astype(BF16), v)
    m_scr[...] = m_new


def _decode_finish(qf, s_bias_new, k_new, v_new, m_scr, l_scr, acc_scr):
    s_new = jnp.sum(qf * k_new, axis=1, keepdims=True) + s_bias_new
    m_old = m_scr[...]
    m_new = jnp.maximum(m_old, s_new)
    alpha = jnp.exp(m_old - m_new)
    p_new = jnp.exp(s_new - m_new)
    l = alpha * l_scr[...] + p_new
    acc = alpha * acc_scr[...] + p_new * v_new
    return _fold_heads(acc / l)


def _dsa_sample_attn_kernel(pt_ref, q_ref, bias_ref, knew_ref, vnew_ref, k_ref, v_ref, o_ref,
                            qf_scr, m_scr, l_scr, acc_scr):
    p = pl.program_id(1)

    @pl.when(p == 0)
    def _():
        qf_scr[...] = _head_rows(q_ref[...] * QK_SCALE, N_HEADS_A)
        m_scr[...] = jnp.full(m_scr.shape, NEG, F32)
        l_scr[...] = jnp.zeros(l_scr.shape, F32)
        acc_scr[...] = jnp.zeros(acc_scr.shape, F32)

    s = _dot_nt(qf_scr[...].astype(BF16), k_ref[...].astype(BF16)) + bias_ref[pl.ds(p, 1), :]
    _decode_step(s, v_ref[...].astype(BF16), m_scr, l_scr, acc_scr)

    @pl.when(p == pl.num_programs(1) - 1)
    def _():
        b_new = bias_ref[pl.ds(p + 1, 1), :][:, 0:1]
        o_ref[...] = _decode_finish(qf_scr[...], b_new, knew_ref[...], vnew_ref[...],
                                    m_scr, l_scr, acc_scr).astype(BF16)


def _dsa_sample_attn(page_table, q, bias, k_new, v_new, cache_k, cache_v):
    nseq, n_pages = page_table.shape
    rowspec = lambda w: pl.BlockSpec((None, 1, w), lambda b, p, pt: (b, 0, 0))
    page = pl.BlockSpec((None, PAGE_SIZE, HA), lambda b, p, pt: (pt[b, p], 0, 0))
    grid_spec = pltpu.PrefetchScalarGridSpec(
        num_scalar_prefetch=1,
        grid=(nseq, n_pages),
        in_specs=[rowspec(HA),
                  pl.BlockSpec((None, n_pages + 1, PAGE_SIZE), lambda b, p, pt: (b, 0, 0)),
                  rowspec(HA), rowspec(HA), page, page],
        out_specs=rowspec(HA),
        scratch_shapes=[pltpu.VMEM((N_HEADS_A, HA), F32), pltpu.VMEM((N_HEADS_A, 1), F32),
                        pltpu.VMEM((N_HEADS_A, 1), F32), pltpu.VMEM((N_HEADS_A, HA), F32)],
    )
    return pl.pallas_call(
        _dsa_sample_attn_kernel,
        grid_spec=grid_spec,
        out_shape=jax.ShapeDtypeStruct((nseq, 1, HA), BF16),
        compiler_params=_cparams("parallel", "arbitrary"),
        name="dsa_sample_attn",
    )(page_table, q, bias, k_new, v_new, cache_k, cache_v)


def _moba_sample_kernel(pt_ref, q_ref, knew_ref, vnew_ref, k_ref, v_ref, o_ref,
                        qf_scr, ksum_scr, gate_scr, mb_scr, lb_scr, accb_scr,
                        m_scr, l_scr, acc_scr, *, pages_per_block):
    p = pl.program_id(1)
    n_pages = pl.num_programs(1)
    n_blocks = gate_scr.shape[0]

    @pl.when(p == 0)
    def _():
        qf_scr[...] = _head_rows(q_ref[...], N_HEADS_B)

    @pl.when(p % pages_per_block == 0)
    def _():
        ksum_scr[...] = jnp.zeros(ksum_scr.shape, F32)
        m_scr[...] = jnp.full(m_scr.shape, NEG, F32)
        l_scr[...] = jnp.zeros(l_scr.shape, F32)
        acc_scr[...] = jnp.zeros(acc_scr.shape, F32)

    k = k_ref[...]
    ksum_scr[...] += jnp.sum(k, axis=0, keepdims=True)
    s = _dot_nt((qf_scr[...] * QK_SCALE).astype(BF16), k.astype(BF16))
    _decode_step(s, v_ref[...].astype(BF16), m_scr, l_scr, acc_scr)

    @pl.when(p % pages_per_block == pages_per_block - 1)
    def _():
        n = p // pages_per_block
        k_mean = ksum_scr[...] / float(pages_per_block * PAGE_SIZE)
        gate_scr[n] = jnp.sum(qf_scr[...] * k_mean, axis=1, keepdims=True)
        mb_scr[n] = m_scr[...]
        lb_scr[n] = l_scr[...]
        accb_scr[n] = acc_scr[...]

    @pl.when(p == n_pages - 1)
    def _():
        gates = [gate_scr[n] for n in range(n_blocks)]
        chosen = [jnp.zeros_like(gates[0]) > 0 for _ in range(n_blocks)]
        for _ in range(min(MOBA_TOPK, n_blocks + 1)):
            mx = functools.reduce(jnp.maximum, gates)
            found = jnp.zeros_like(mx) > 0
            for n in range(n_blocks):
                pick = (gates[n] == mx) & ~found & (mx > -jnp.inf)
                found = found | pick
                chosen[n] = chosen[n] | pick
                gates[n] = jnp.where(pick, -jnp.inf, gates[n])
        qf = qf_scr[...] * QK_SCALE
        s_new = jnp.sum(qf * knew_ref[...], axis=1, keepdims=True)
        m_tot = s_new
        for n in range(n_blocks):
            m_tot = jnp.maximum(m_tot, jnp.where(chosen[n], mb_scr[n], NEG))
        p_new = jnp.exp(s_new - m_tot)
        l = p_new
        acc = p_new * vnew_ref[...]
        for n in range(n_blocks):
            wgt = jnp.where(chosen[n], jnp.exp(mb_scr[n] - m_tot), 0.0)
            l = l + wgt * lb_scr[n]
            acc = acc + wgt * accb_scr[n]
        o_ref[...] = _fold_heads(acc / l).astype(BF16)


def _moba_sample(page_table, q, k_new, v_new, cache_k, cache_v):
    nseq, n_pages = page_table.shape
    ppb = MOBA_BLOCK // PAGE_SIZE
    n_blocks = n_pages // ppb
    rowspec = lambda w: pl.BlockSpec((None, 1, w), lambda b, p, pt: (b, 0, 0))
    page = pl.BlockSpec((None, PAGE_SIZE, HB), lambda b, p, pt: (pt[b, p], 0, 0))
    grid_spec = pltpu.PrefetchScalarGridSpec(
        num_scalar_prefetch=1,
        grid=(nseq, n_pages),
        in_specs=[rowspec(HB), rowspec(HB), rowspec(HB), page, page],
        out_specs=rowspec(HB),
        scratch_shapes=[pltpu.VMEM((N_HEADS_B, HB), F32), pltpu.VMEM((1, HB), F32),
                        pltpu.VMEM((n_blocks, N_HEADS_B, 1), F32),
                        pltpu.VMEM((n_blocks, N_HEADS_B, 1), F32),
                        pltpu.VMEM((n_blocks, N_HEADS_B, 1), F32),
                        pltpu.VMEM((n_blocks, N_HEADS_B, HB), F32),
                        pltpu.VMEM((N_HEADS_B, 1), F32), pltpu.VMEM((N_HEADS_B, 1), F32),
                        pltpu.VMEM((N_HEADS_B, HB), F32)],
    )
    return pl.pallas_call(
        functools.partial(_moba_sample_kernel, pages_per_block=ppb),
        grid_spec=grid_spec,
        out_shape=jax.ShapeDtypeStruct((nseq, 1, HB), BF16),
        compiler_params=_cparams("parallel", "arbitrary"),
        name="moba_sample",
    )(page_table, q, k_new, v_new, cache_k, cache_v)


def _fox_sample_bias_kernel(pt_ref, lnew_ref, *rest, n_pages):
    page_refs = rest[:n_pages]
    bias_ref, bnew_ref = rest[n_pages], rest[n_pages + 1]
    r = lax.broadcasted_iota(I32, (PAGE_SIZE, PAGE_SIZE), 0)
    cc = lax.broadcasted_iota(I32, (PAGE_SIZE, PAGE_SIZE), 1)
    tril = jnp.where(cc <= r, 1.0, 0.0).astype(BF16)
    carry = jnp.zeros((1, N_HEADS_C), F32)
    cums = []
    for p in range(n_pages):
        hi, mid, lo = _split3(page_refs[p][...])
        cum = (_dot(tril, hi) + _dot(tril, mid) + _dot(tril, lo)) + carry
        cums.append(cum)
        carry = cum[PAGE_SIZE - 1:PAGE_SIZE, :]
    cum_q = carry + lnew_ref[...][:, 0:N_HEADS_C]
    for p in range(n_pages):
        bias_ref[p * PAGE_SIZE:(p + 1) * PAGE_SIZE, :] = cum_q - cums[p]
    bnew_ref[...] = cum_q - cum_q


def _fox_sample_bias(page_table, logf_new, cache_logf):
    nseq, n_pages = page_table.shape
    page = lambda p: pl.BlockSpec((None, PAGE_SIZE, N_HEADS_C), lambda b, pt: (pt[b, p], 0, 0))
    grid_spec = pltpu.PrefetchScalarGridSpec(
        num_scalar_prefetch=1,
        grid=(nseq,),
        in_specs=[pl.BlockSpec((None, 1, LANES), lambda b, pt: (b, 0, 0))] + [page(p) for p in range(n_pages)],
        out_specs=[pl.BlockSpec((None, n_pages * PAGE_SIZE, N_HEADS_C), lambda b, pt: (b, 0, 0)),
                   pl.BlockSpec((None, 1, N_HEADS_C), lambda b, pt: (b, 0, 0))],
    )
    return pl.pallas_call(
        functools.partial(_fox_sample_bias_kernel, n_pages=n_pages),
        grid_spec=grid_spec,
        out_shape=[jax.ShapeDtypeStruct((nseq, n_pages * PAGE_SIZE, N_HEADS_C), F32),
                   jax.ShapeDtypeStruct((nseq, 1, N_HEADS_C), F32)],
        compiler_params=_cparams("parallel"),
        name="fox_sample_bias",
    )(page_table, logf_new, *([cache_logf] * n_pages))


def _fox_sample_attn_kernel(pt_ref, q_ref, bias_ref, bnew_ref, knew_ref, vnew_ref, k_ref, v_ref, o_ref,
                            qf_scr, m_scr, l_scr, acc_scr):
    p = pl.program_id(1)

    @pl.when(p == 0)
    def _():
        qf_scr[...] = _head_rows(q_ref[...] * QK_SCALE, N_HEADS_C)
        m_scr[...] = jnp.full(m_scr.shape, NEG, F32)
        l_scr[...] = jnp.zeros(l_scr.shape, F32)
        acc_scr[...] = jnp.zeros(acc_scr.shape, F32)

    s = _dot_nt(qf_scr[...].astype(BF16), k_ref[...].astype(BF16)) + bias_ref[...]
    _decode_step(s, v_ref[...].astype(BF16), m_scr, l_scr, acc_scr)

    @pl.when(p == pl.num_programs(1) - 1)
    def _():
        o_ref[...] = _decode_finish(qf_scr[...], bnew_ref[...], knew_ref[...], vnew_ref[...],
                                    m_scr, l_scr, acc_scr).astype(BF16)


def _fox_sample_attn(page_table, q, bias_t, bias_new, k_new, v_new, cache_k, cache_v):
    nseq, n_pages = page_table.shape
    rowspec = lambda w: pl.BlockSpec((None, 1, w), lambda b, p, pt: (b, 0, 0))
    page = pl.BlockSpec((None, PAGE_SIZE, HC), lambda b, p, pt: (pt[b, p], 0, 0))
    grid_spec = pltpu.PrefetchScalarGridSpec(
        num_scalar_prefetch=1,
        grid=(nseq, n_pages),
        in_specs=[rowspec(HC),
                  pl.BlockSpec((None, N_HEADS_C, PAGE_SIZE), lambda b, p, pt: (b, 0, p)),
                  pl.BlockSpec((None, N_HEADS_C, 1), lambda b, p, pt: (b, 0, 0)),
                  rowspec(HC), rowspec(HC), page, page],
        out_specs=rowspec(HC),
        scratch_shapes=[pltpu.VMEM((N_HEADS_C, HC), F32), pltpu.VMEM((N_HEADS_C, 1), F32),
                        pltpu.VMEM((N_HEADS_C, 1), F32), pltpu.VMEM((N_HEADS_C, HC), F32)],
    )
    return pl.pallas_call(
        _fox_sample_attn_kernel,
        grid_spec=grid_spec,
        out_shape=jax.ShapeDtypeStruct((nseq, 1, HC), BF16),
        compiler_params=_cparams("parallel", "arbitrary"),
        name="fox_sample_attn",
    )(page_table, q, bias_t, bias_new, k_new, v_new, cache_k, cache_v)


def _pad_cols(w, width):
    return jnp.pad(w, ((0, 0), (0, width - w.shape[1])))


def _rope_inv_freq():
    half = HEAD_DIM // 8
    inv = ROPE_THETA ** (-jnp.arange(half, dtype=F32) / half)
    per_head = jnp.concatenate([inv, inv, jnp.zeros((HEAD_DIM - 2 * half,), F32)])
    return jnp.tile(per_head, LANES // HEAD_DIM)[None, :]


def _ab_layer_prompt(x2d, w16, w_out16, invf, g, b, batch, seq):
    n = batch * seq
    pos = jnp.tile(jnp.arange(seq, dtype=F32), batch)[:, None]
    qa, ka, va, qb, kb, vb, qi, tail, ka16, va16, kb16, vb16 = _ab_project(x2d, w16, pos, invf, PROJ_TM)
    k_idx = tail[:, :IDX_DIM]
    nk = seq // DSA_TK
    kit = k_idx.reshape(batch, nk, DSA_TK, IDX_DIM).transpose(0, 1, 3, 2)
    kit_hi, kit_lo = _split2(kit)
    o_a = _dsa_prompt(qa, qi, tail, kit_hi, kit_lo, ka16, va16, batch, seq, DSA_TQ, DSA_TK)
    kmean = _block_means(kb).reshape(batch, seq // MOBA_BLOCK, HB)
    o_b = _moba_prompt(qb, kmean, kb16, vb16, batch, seq)
    o = jnp.concatenate([o_a, o_b], axis=1)
    y = _out_proj_ln(o, w_out16, x2d, g, b, PROJ_TM)
    return y, (ka, va, k_idx, kb, vb)


def _ab_layer_sample(x2d, w16, w_out16, invf, g, b, past_len, page_table,
                     cache_a_k, cache_a_v, cache_a_idx, cache_b_k, cache_b_v):
    nseq = x2d.shape[0]
    n_pages = page_table.shape[1]
    pos = jnp.full((nseq, 1), past_len, F32)
    qa, ka, va, qb, kb, vb, qi, tail, _, _, _, _ = _ab_project(x2d, w16, pos, invf, nseq)
    k_idx = tail[:, :IDX_DIM]
    w_i = tail[:, IDX_DIM:IDX_DIM + IDX_HEADS]
    r3 = lambda a: a.reshape(nseq, 1, a.shape[1])
    scores = _dsa_sample_scores(page_table, qi.reshape(nseq, IDX_HEADS, IDX_DIM),
                                w_i.reshape(nseq, IDX_HEADS, 1), r3(tail), cache_a_idx)
    n_keys = n_pages * PAGE_SIZE + 1
    valid = (jnp.arange((n_pages + 1) * PAGE_SIZE) < n_keys).astype(F32)
    valid_t = jnp.broadcast_to(valid.reshape(n_pages + 1, 1, PAGE_SIZE), (n_pages + 1, nseq, PAGE_SIZE))
    bias_t = _select_topk_bias(scores.transpose(1, 0, 2), valid_t, min(DSA_TOPK, n_keys // 4))
    o_a = _dsa_sample_attn(page_table, r3(qa), bias_t.transpose(1, 0, 2), r3(ka), r3(va),
                           cache_a_k.reshape(-1, PAGE_SIZE, HA), cache_a_v.reshape(-1, PAGE_SIZE, HA))
    o_b = _moba_sample(page_table, r3(qb), r3(kb), r3(vb),
                       cache_b_k.reshape(-1, PAGE_SIZE, HB), cache_b_v.reshape(-1, PAGE_SIZE, HB))
    o = jnp.concatenate([o_a.reshape(nseq, HA), o_b.reshape(nseq, HB)], axis=1)
    y = _out_proj_ln(o, w_out16, x2d, g, b, nseq)
    return y, (ka, va, k_idx, kb, vb)


def _fox_layer_prompt(x2d, w16, bf_pad, w_out16, g, b, batch, seq):
    q, k, v, logf, cum, k16, v16 = _fox_project(x2d, w16, bf_pad, PROJ_TM, seq)
    t = FOX_T
    npair = N_HEADS_C // 2
    cum_h = cum[:, :N_HEADS_C].reshape(batch, seq, npair, 2)
    cum_q = cum_h.transpose(0, 2, 1, 3)
    cum_k = cum_h.reshape(batch, seq // t, t, npair, 2).transpose(0, 3, 1, 4, 2)
    o = _fox_prompt(q, cum_q, cum_k, k16, v16, batch, seq, t)
    y = _out_proj_ln(o, w_out16, x2d, g, b, PROJ_TM)
    return y, (k, v, logf[:, :N_HEADS_C])


def _fox_layer_sample(x2d, w16, bf_pad, w_out16, g, b, page_table, cache_c_k, cache_c_v, cache_c_logf):
    nseq = x2d.shape[0]
    q, k, v, logf, _, _, _ = _fox_project(x2d, w16, bf_pad, nseq, nseq)
    r3 = lambda a: a.reshape(nseq, 1, a.shape[1])
    bias, bias_new = _fox_sample_bias(page_table, r3(logf), cache_c_logf)
    o = _fox_sample_attn(page_table, r3(q), bias.transpose(0, 2, 1), bias_new.transpose(0, 2, 1),
                         r3(k), r3(v), cache_c_k.reshape(-1, PAGE_SIZE, HC),
                         cache_c_v.reshape(-1, PAGE_SIZE, HC))
    y = _out_proj_ln(o.reshape(nseq, HC), w_out16, x2d, g, b, nseq)
    return y, (k, v, logf[:, :N_HEADS_C])


def _moe_layer(x2d, rwt_hi, rwt_lo, rb, wg16, wu16, wd16, g, b, tm):
    gates_t = _router(x2d, rwt_hi, rwt_lo, rb, tm)
    return _moe_ln(x2d, gates_t.T, wg16, wu16, wd16, g, b, tm)


def kernel(x_prompt, x_sample, cache_a_k, cache_a_v, cache_a_idx, cache_b_k, cache_b_v, cache_c_k, cache_c_v, cache_c_logf, page_table, w_in_ab, w_out_ab, w_in_fox, b_forget, w_out_fox, ln_mix_g, ln_mix_b, ln_ffn_g, ln_ffn_b, router_w, router_bias, exp_w_gate, exp_w_up, exp_w_down):
    batch, seq, _ = x_prompt.shape
    nseq = x_sample.shape[0]
    past_len = page_table.shape[1] * PAGE_SIZE

    w_ab16 = _pad_cols(w_in_ab, 7 * HA + LANES).astype(BF16)
    w_out_ab16 = w_out_ab.astype(BF16)
    w_fox16 = _pad_cols(w_in_fox, 3 * HC + LANES).astype(BF16)
    w_out_fox16 = w_out_fox.astype(BF16)
    bf_pad = jnp.pad(b_forget, (0, LANES - N_HEADS_C))[None, :]
    invf = _rope_inv_freq()
    rwt_hi, rwt_lo = _split2(router_w.T)
    rb = router_bias[:, None]
    wg16, wu16, wd16 = exp_w_gate.astype(BF16), exp_w_up.astype(BF16), exp_w_down.astype(BF16)
    row = lambda a, i: a[i][None, :]

    xp = x_prompt.reshape(batch * seq, D_MODEL)
    xs = x_sample.reshape(nseq, D_MODEL)

    xp, (pa_k, pa_v, pa_idx, pb_k, pb_v) = _ab_layer_prompt(
        xp, w_ab16, w_out_ab16, invf, row(ln_mix_g, 0), row(ln_mix_b, 0), batch, seq)
    xs, (sa_k, sa_v, sa_idx, sb_k, sb_v) = _ab_layer_sample(
        xs, w_ab16, w_out_ab16, invf, row(ln_mix_g, 0), row(ln_mix_b, 0), past_len, page_table,
        cache_a_k, cache_a_v, cache_a_idx, cache_b_k, cache_b_v)
    xp = _moe_layer(xp, rwt_hi, rwt_lo, rb, wg16[0], wu16[0], wd16[0], row(ln_ffn_g, 0), row(ln_ffn_b, 0), MOE_TM)
    xs = _moe_layer(xs, rwt_hi, rwt_lo, rb, wg16[0], wu16[0], wd16[0], row(ln_ffn_g, 0), row(ln_ffn_b, 0), nseq)

    xp, (pc_k, pc_v, pc_logf) = _fox_layer_prompt(
        xp, w_fox16, bf_pad, w_out_fox16, row(ln_mix_g, 1), row(ln_mix_b, 1), batch, seq)
    xs, (sc_k, sc_v, sc_logf) = _fox_layer_sample(
        xs, w_fox16, bf_pad, w_out_fox16, row(ln_mix_g, 1), row(ln_mix_b, 1), page_table,
        cache_c_k, cache_c_v, cache_c_logf)
    xp = _moe_layer(xp, rwt_hi, rwt_lo, rb, wg16[1], wu16[1], wd16[1], row(ln_ffn_g, 1), row(ln_ffn_b, 1), MOE_TM)
    xs = _moe_layer(xs, rwt_hi, rwt_lo, rb, wg16[1], wu16[1], wd16[1], row(ln_ffn_g, 1), row(ln_ffn_b, 1), nseq)

    hd = lambda a, nh, lead: a.reshape(*lead, nh, HEAD_DIM)
    lp, ls = (batch, seq), (nseq, 1)
    return (xp.reshape(batch, seq, D_MODEL), xs.reshape(nseq, 1, D_MODEL),
            hd(pa_k, N_HEADS_A, lp), hd(pa_v, N_HEADS_A, lp), pa_idx.reshape(batch, seq, IDX_DIM),
            hd(pb_k, N_HEADS_B, lp), hd(pb_v, N_HEADS_B, lp),
            hd(pc_k, N_HEADS_C, lp), hd(pc_v, N_HEADS_C, lp), pc_logf.reshape(batch, seq, N_HEADS_C),
            hd(sa_k, N_HEADS_A, ls), hd(sa_v, N_HEADS_A, ls), sa_idx.reshape(nseq, 1, IDX_DIM),
            hd(sb_k, N_HEADS_B, ls), hd(sb_v, N_HEADS_B, ls),
            hd(sc_k, N_HEADS_C, ls), hd(sc_v, N_HEADS_C, ls), sc_logf.reshape(nseq, 1, N_HEADS_C))
```

```python
import functools

import jax
import jax.numpy as jnp
from jax import lax
from jax.experimental import pallas as pl
from jax.experimental.pallas import tpu as pltpu

F32 = jnp.float32
BF16 = jnp.bfloat16
I32 = jnp.int32

D_MODEL = 1024
DEPTH = 2
PAGE_SIZE = 128
HEAD_DIM = 64
N_HEADS_A = 8
N_HEADS_B = 8
N_HEADS_C = 16
IDX_HEADS = 8
IDX_DIM = 64
DSA_TOPK = 256
MOBA_BLOCK = 256
MOBA_TOPK = 3
ROPE_THETA = 500000.0
N_EXPERTS = 16
N_GROUPS = 4
EXPERTS_PER_GROUP = N_EXPERTS // N_GROUPS
D_EXPERT = 512
ALPHA = (2 * DEPTH) ** 0.25
LN_EPS = 1e-5
HA = N_HEADS_A * HEAD_DIM
HB = N_HEADS_B * HEAD_DIM
HC = N_HEADS_C * HEAD_DIM
QK_SCALE = HEAD_DIM ** -0.5
IDX_SCALE = IDX_DIM ** -0.5

LANES = 128
NEG = -1e30
INT_MIN = -2 ** 31
VMEM_LIMIT = 56 * 2 ** 20

PROJ_TM = 256
DSA_TQ = 256
DSA_TK = 512
MOBA_T = MOBA_BLOCK
FOX_TQ = 256
FOX_TK = 512
MOE_TM = 512


def _cparams(*sem):
    return pltpu.CompilerParams(dimension_semantics=sem, vmem_limit_bytes=VMEM_LIMIT)


def _dot(a, b):
    return jnp.dot(a, b, preferred_element_type=F32)


def _dot_nt(a, b):
    return lax.dot_general(a, b, (((1,), (1,)), ((), ())), preferred_element_type=F32)


def _split2(x):
    hi = x.astype(BF16)
    lo = (x - hi.astype(F32)).astype(BF16)
    return hi, lo


def _split3(x):
    hi = x.astype(BF16)
    r = x - hi.astype(F32)
    mid = r.astype(BF16)
    lo = (r - mid.astype(F32)).astype(BF16)
    return hi, mid, lo


def _dot3(a_hi, a_lo, b_hi, b_lo, dot):
    return dot(a_hi, b_hi) + (dot(a_hi, b_lo) + dot(a_lo, b_hi))


def _layer_norm(z, g, b):
    mu = jnp.mean(z, axis=-1, keepdims=True)
    d = z - mu
    var = jnp.mean(d * d, axis=-1, keepdims=True)
    return d * lax.rsqrt(var + LN_EPS) * g + b


def _log_sigmoid(z):
    return -(jnp.maximum(-z, 0.0) + jnp.log1p(jnp.exp(-jnp.abs(z))))


def _rotary_tables(pos_ref, invf_ref):
    ang = pos_ref[...] * invf_ref[...]
    c = jnp.cos(ang)
    s = jnp.sin(ang)
    f = lax.broadcasted_iota(I32, ang.shape, 1) % HEAD_DIM
    s_up = jnp.where(f < 8, -s, 0.0)
    s_dn = jnp.where(f >= 8, s, 0.0)
    return c, s_up, s_dn


def _rotate(h, c, s_up, s_dn):
    outs = []
    for j in range(h.shape[1] // LANES):
        hc = h[:, j * LANES:(j + 1) * LANES]
        outs.append(hc * c + pltpu.roll(hc, LANES - 8, 1) * s_up + pltpu.roll(hc, 8, 1) * s_dn)
    return outs[0] if len(outs) == 1 else jnp.concatenate(outs, axis=1)


def _ab_proj_kernel(x_ref, w_ref, pos_ref, invf_ref,
                    qa_ref, ka_ref, va_ref, qb_ref, kb_ref, vb_ref, qi_ref, tail_ref,
                    ka16_ref, va16_ref, kb16_ref, vb16_ref):
    xb = x_ref[...].astype(BF16)
    c, s_up, s_dn = _rotary_tables(pos_ref, invf_ref)

    def seg(j, width=HA):
        return _dot(xb, w_ref[:, j * HA:j * HA + width])

    qa_ref[...] = _rotate(seg(0), c, s_up, s_dn)
    ka = _rotate(seg(1), c, s_up, s_dn)
    ka_ref[...] = ka
    ka16_ref[...] = ka.astype(BF16)
    va = seg(2)
    va_ref[...] = va
    va16_ref[...] = va.astype(BF16)
    qb_ref[...] = _rotate(seg(3), c, s_up, s_dn)
    kb = _rotate(seg(4), c, s_up, s_dn)
    kb_ref[...] = kb
    kb16_ref[...] = kb.astype(BF16)
    vb = seg(5)
    vb_ref[...] = vb
    vb16_ref[...] = vb.astype(BF16)
    qi_ref[...] = _rotate(seg(6), c, s_up, s_dn)
    t = seg(7, LANES)
    lane = lax.broadcasted_iota(I32, t.shape, 1)
    is_key = lane < IDX_DIM
    ct = jnp.where(is_key, c, IDX_HEADS ** -0.5)
    tail_ref[...] = (t * ct + pltpu.roll(t, LANES - 8, 1) * jnp.where(is_key, s_up, 0.0)
                     + pltpu.roll(t, 8, 1) * jnp.where(is_key, s_dn, 0.0))


def _ab_project(x2d, w16, pos, invf, tm):
    n = x2d.shape[0]
    wide = jax.ShapeDtypeStruct((n, HA), F32)
    wide16 = jax.ShapeDtypeStruct((n, HA), BF16)
    row = lambda w: pl.BlockSpec((tm, w), lambda i: (i, 0))
    full = lambda a: pl.BlockSpec(a.shape, lambda i: (0, 0))
    return pl.pallas_call(
        _ab_proj_kernel,
        grid=(n // tm,),
        in_specs=[row(D_MODEL), full(w16), row(1), full(invf)],
        out_specs=[row(HA)] * 7 + [row(LANES)] + [row(HA)] * 4,
        out_shape=[wide] * 7 + [jax.ShapeDtypeStruct((n, LANES), F32)] + [wide16] * 4,
        compiler_params=_cparams("parallel"),
        name="ab_project",
    )(x2d, w16, pos, invf)


def _fox_proj_kernel(x_ref, w_ref, bf_ref, q_ref, k_ref, v_ref, logf_ref, cum_ref,
                     k16_ref, v16_ref, carry_ref, *, tiles_per_seq):
    i = pl.program_id(0)
    xb = x_ref[...].astype(BF16)
    q_ref[...] = _dot(xb, w_ref[:, 0:HC])
    k = _dot(xb, w_ref[:, HC:2 * HC])
    k_ref[...] = k
    k16_ref[...] = k.astype(BF16)
    v = _dot(xb, w_ref[:, 2 * HC:3 * HC])
    v_ref[...] = v
    v16_ref[...] = v.astype(BF16)
    f = _dot(xb, w_ref[:, 3 * HC:3 * HC + LANES])
    logf = _log_sigmoid(f + bf_ref[...])
    logf_ref[...] = logf

    @pl.when(i % tiles_per_seq == 0)
    def _():
        carry_ref[...] = jnp.zeros_like(carry_ref)

    tm = logf.shape[0]
    r = lax.broadcasted_iota(I32, (tm, tm), 0)
    cc = lax.broadcasted_iota(I32, (tm, tm), 1)
    tril = jnp.where(cc <= r, 1.0, 0.0).astype(BF16)
    hi, mid, lo = _split3(logf)
    cum = (_dot(tril, hi) + _dot(tril, mid) + _dot(tril, lo)) + carry_ref[...]
    cum_ref[...] = cum
    carry_ref[...] = cum[tm - 1:tm, :]


def _fox_project(x2d, w16, bf_pad, tm, rows_per_seq):
    n = x2d.shape[0]
    wide = jax.ShapeDtypeStruct((n, HC), F32)
    wide16 = jax.ShapeDtypeStruct((n, HC), BF16)
    small = jax.ShapeDtypeStruct((n, LANES), F32)
    row = lambda w: pl.BlockSpec((tm, w), lambda i: (i, 0))
    full = lambda a: pl.BlockSpec(a.shape, lambda i: (0, 0))
    return pl.pallas_call(
        functools.partial(_fox_proj_kernel, tiles_per_seq=rows_per_seq // tm),
        grid=(n // tm,),
        in_specs=[row(D_MODEL), full(w16), full(bf_pad)],
        out_specs=[row(HC)] * 3 + [row(LANES)] * 2 + [row(HC)] * 2,
        out_shape=[wide] * 3 + [small] * 2 + [wide16] * 2,
        scratch_shapes=[pltpu.VMEM((1, LANES), F32)],
        compiler_params=_cparams("arbitrary"),
        name="fox_project",
    )(x2d, w16, bf_pad)


def _out_ln_kernel(o_ref, w_ref, x_ref, g_ref, b_ref, y_ref):
    m = _dot(o_ref[...], w_ref[...])
    y_ref[...] = _layer_norm(ALPHA * x_ref[...] + m, g_ref[...], b_ref[...])


def _out_proj_ln(o16, w16, x2d, g, b, tm):
    n, k = o16.shape
    row = lambda w: pl.BlockSpec((tm, w), lambda i: (i, 0))
    full = lambda a: pl.BlockSpec(a.shape, lambda i: (0, 0))
    return pl.pallas_call(
        _out_ln_kernel,
        grid=(n // tm,),
        in_specs=[row(k), full(w16), row(D_MODEL), full(g), full(b)],
        out_specs=row(D_MODEL),
        out_shape=jax.ShapeDtypeStruct((n, D_MODEL), F32),
        compiler_params=_cparams("parallel"),
        name="out_proj_ln",
    )(o16, w16, x2d, g, b)


def _softmax_init(m_scr, l_scr, acc_scr):
    m_scr[...] = jnp.full(m_scr.shape, NEG, F32)
    l_scr[...] = jnp.zeros(l_scr.shape, F32)
    acc_scr[...] = jnp.zeros(acc_scr.shape, F32)


def _softmax_step(h, s, vt, m_scr, l_scr, acc_scr):
    m_old = m_scr[h]
    m_new = jnp.maximum(m_old, jnp.max(s, axis=0, keepdims=True))
    alpha = jnp.exp(m_old - m_new)
    p = jnp.exp(s - m_new)
    l_scr[h] = alpha * l_scr[h] + jnp.sum(p, axis=0, keepdims=True)
    acc_scr[h] = alpha * acc_scr[h] + _dot(vt, p.astype(BF16))
    m_scr[h] = m_new


def _pair_masks(shape):
    lane = lax.broadcasted_iota(I32, shape, 1)
    return lane < HEAD_DIM


def _store_pair_queries(q, qm_scr, base):
    lo_head = _pair_masks(q.shape)
    qs = (q * QK_SCALE).astype(BF16)
    zero = jnp.zeros_like(qs)
    qm_scr[base] = jnp.where(lo_head, qs, zero)
    qm_scr[base + 1] = jnp.where(lo_head, zero, qs)


def _pair_output(hp, l_scr, acc_scr):
    a0 = acc_scr[2 * hp] / l_scr[2 * hp]
    a1 = acc_scr[2 * hp + 1] / l_scr[2 * hp + 1]
    first = lax.broadcasted_iota(I32, a0.shape, 0) < HEAD_DIM
    return jnp.where(first, a0, a1).T.astype(BF16)


def _key_query_index(tk, tq, k0, q0):
    kidx = k0 + lax.broadcasted_iota(I32, (tk, tq), 0)
    qidx = q0 + lax.broadcasted_iota(I32, (tk, tq), 1)
    return kidx, qidx


def _sortable_key(x):
    b = pltpu.bitcast(x, I32)
    return jnp.where(b < 0, INT_MIN - b, b)


def _count_keys(key_scr, n_tiles, pred):
    _, tk, tq = key_scr.shape

    def body(j, acc):
        for cidx in range(tk // 8):
            chunk = key_scr[j, cidx * 8:(cidx + 1) * 8, :]
            acc = acc + jnp.where(pred(chunk, j * tk + cidx * 8), 1, 0)
        return acc

    acc = lax.fori_loop(0, n_tiles, body, jnp.zeros((8, tq), I32))
    return jnp.sum(acc, axis=0, keepdims=True)


def _topk_threshold(key_scr, n_tiles, topk, key_bits):
    _, _, tq = key_scr.shape
    rows8 = lambda x: jnp.broadcast_to(x, (8, tq))

    def bit_body(i, t):
        cand = rows8(t + jnp.left_shift(jnp.int32(1), 31 - i))
        cnt = _count_keys(key_scr, n_tiles, lambda k, k0: k >= cand)
        return jnp.where(cnt >= topk, cand[0:1, :], t)

    t = lax.fori_loop(0, 32, bit_body, jnp.full((1, tq), INT_MIN, I32))
    t8 = rows8(t)
    n_gt = _count_keys(key_scr, n_tiles, lambda k, k0: k > t8)
    n_eq = _count_keys(key_scr, n_tiles, lambda k, k0: k == t8)
    need = topk - n_gt
    excess = jnp.max(jnp.where(n_eq > need, 1, 0)) > 0

    def tie_search():
        sub = lax.broadcasted_iota(I32, (8, tq), 0)

        def body(i, x):
            cand = rows8(x + jnp.left_shift(jnp.int32(1), key_bits - 1 - i))
            cnt = _count_keys(key_scr, n_tiles, lambda k, k0: (k == t8) & (sub + k0 < cand))
            return jnp.where(cnt < need, cand[0:1, :], x)

        return lax.fori_loop(0, key_bits, body, jnp.zeros((1, tq), I32))

    c = lax.cond(excess, tie_search, lambda: jnp.full((1, tq), 2 ** 31 - 1, I32))
    c = jnp.where(t == INT_MIN, -1, c)
    return t, c


def _dsa_prompt_kernel(qa_ref, qi_ref, wt_ref, kih_ref, kil_ref, k_ref, vt_ref, o_ref,
                       key_scr, qih_scr, qil_scr, qm_scr, m_scr, l_scr, acc_scr,
                       *, tq, tk, topk, key_bits):
    qt = pl.program_id(1)
    q0 = qt * tq
    n_tiles = (q0 + tq + tk - 1) // tk

    qi = qi_ref[...]
    for h in range(IDX_HEADS):
        hi, lo = _split2(qi[:, h * IDX_DIM:(h + 1) * IDX_DIM])
        qih_scr[h] = hi
        qil_scr[h] = lo
    wt = wt_ref[...]

    def score_tile(j, diagonal):
        kh = kih_ref[pl.ds(j * tk, tk), :]
        kl = kil_ref[pl.ds(j * tk, tk), :]
        sc = jnp.zeros((tk, tq), F32)
        for h in range(IDX_HEADS):
            d = _dot3(kh, kl, qih_scr[h], qil_scr[h], _dot_nt)
            sc = sc + wt[h:h + 1, :] * jnp.maximum(d, 0.0)
        key = _sortable_key(sc * IDX_SCALE)
        if diagonal:
            kidx, qidx = _key_query_index(tk, tq, j * tk, q0)
            key = jnp.where(kidx <= qidx, key, INT_MIN)
        key_scr[j] = key

    lax.fori_loop(0, n_tiles - 1, lambda j, _: (score_tile(j, False), 0)[1], 0)
    score_tile(n_tiles - 1, True)

    t, c = _topk_threshold(key_scr, n_tiles, topk, key_bits)

    qa = qa_ref[...]
    for hp in range(N_HEADS_A // 2):
        _store_pair_queries(qa[:, hp * LANES:(hp + 1) * LANES], qm_scr, 2 * hp)
    _softmax_init(m_scr, l_scr, acc_scr)

    def attn_tile(j, _):
        kt = key_scr[j]
        kidx = j * tk + lax.broadcasted_iota(I32, (tk, tq), 0)
        msk = (kt > t) | ((kt == t) & (kidx <= c))
        for hp in range(N_HEADS_A // 2):
            k = k_ref[pl.ds(j * tk, tk), hp * LANES:(hp + 1) * LANES]
            vt = vt_ref[hp, j]
            for h in (2 * hp, 2 * hp + 1):
                s = jnp.where(msk, _dot_nt(k, qm_scr[h]), NEG)
                _softmax_step(h, s, vt, m_scr, l_scr, acc_scr)
        return 0

    lax.fori_loop(0, n_tiles, attn_tile, 0)
    for hp in range(N_HEADS_A // 2):
        o_ref[:, hp * LANES:(hp + 1) * LANES] = _pair_output(hp, l_scr, acc_scr)


def _dsa_prompt(qa, qi, wt, ki_hi, ki_lo, k16, vt16, batch, seq, tq, tk):
    nq = seq // tq
    nk = seq // tk
    npair = N_HEADS_A // 2
    topk = min(DSA_TOPK, seq // 4)
    key_bits = max(1, (seq - 1).bit_length())
    once = pl.Buffered(1)
    qrow = lambda w: pl.BlockSpec((tq, w), lambda b, i: (b * nq + i, 0))
    return pl.pallas_call(
        functools.partial(_dsa_prompt_kernel, tq=tq, tk=tk, topk=topk, key_bits=key_bits),
        grid=(batch, nq),
        in_specs=[
            qrow(HA), qrow(HA),
            pl.BlockSpec((None, IDX_HEADS, tq), lambda b, i: (b, 0, i)),
            pl.BlockSpec((seq, IDX_DIM), lambda b, i: (b, 0), pipeline_mode=once),
            pl.BlockSpec((seq, IDX_DIM), lambda b, i: (b, 0), pipeline_mode=once),
            pl.BlockSpec((seq, HA), lambda b, i: (b, 0), pipeline_mode=once),
            pl.BlockSpec((None, npair, nk, LANES, tk), lambda b, i: (b, 0, 0, 0, 0), pipeline_mode=once),
        ],
        out_specs=qrow(HA),
        out_shape=jax.ShapeDtypeStruct((batch * seq, HA), BF16),
        scratch_shapes=[
            pltpu.VMEM((nk, tk, tq), I32),
            pltpu.VMEM((IDX_HEADS, tq, IDX_DIM), BF16),
            pltpu.VMEM((IDX_HEADS, tq, IDX_DIM), BF16),
            pltpu.VMEM((N_HEADS_A, tq, LANES), BF16),
            pltpu.VMEM((N_HEADS_A, 1, tq), F32),
            pltpu.VMEM((N_HEADS_A, 1, tq), F32),
            pltpu.VMEM((N_HEADS_A, LANES, tq), F32),
        ],
        compiler_params=_cparams("parallel", "parallel"),
        name="dsa_prompt",
    )(qa, qi, wt, ki_hi, ki_lo, k16, vt16)


def _block_mean_kernel(k_ref, o_ref):
    o_ref[...] = jnp.mean(k_ref[...], axis=0, keepdims=True)[None]


def _block_means(k2d):
    n = k2d.shape[0]
    nb = n // MOBA_BLOCK
    return pl.pallas_call(
        _block_mean_kernel,
        grid=(nb,),
        in_specs=[pl.BlockSpec((MOBA_BLOCK, HB), lambda i: (i, 0))],
        out_specs=pl.BlockSpec((1, 1, HB), lambda i: (i, 0, 0)),
        out_shape=jax.ShapeDtypeStruct((nb, 1, HB), F32),
        compiler_params=_cparams("parallel"),
        name="moba_block_means",
    )(k2d)


def _moba_prompt_kernel(q_ref, km_ref, k_ref, vt_ref, o_ref,
                        sel_scr, qm_scr, m_scr, l_scr, acc_scr, *, t):
    qt = pl.program_id(2)
    q = q_ref[...]
    nb = km_ref.shape[0]
    lo_head = _pair_masks(q.shape)
    km_hi, km_lo = _split2(km_ref[...])
    blk = lax.broadcasted_iota(I32, (nb, t), 0)
    for hh in range(2):
        qh = jnp.where(lo_head if hh == 0 else ~lo_head, q, 0.0)
        q_hi, q_lo = _split2(qh)
        g = _dot3(km_hi, km_lo, q_hi, q_lo, _dot_nt)
        g = jnp.where(blk < qt, g, -jnp.inf)
        sel = jnp.zeros((nb, t), F32)
        for _ in range(MOBA_TOPK):
            mx = jnp.max(g, axis=0, keepdims=True)
            first = jnp.min(jnp.where(g == mx, blk, nb), axis=0, keepdims=True)
            pick = (blk == first) & (mx > -jnp.inf)
            sel = jnp.where(pick, 1.0, sel)
            g = jnp.where(blk == first, -jnp.inf, g)
        sel_scr[hh] = sel
    _store_pair_queries(q, qm_scr, 0)
    _softmax_init(m_scr, l_scr, acc_scr)

    def past_tile(j, _):
        k = k_ref[pl.ds(j * t, t), :]
        vt = vt_ref[j]
        for hh in range(2):
            chosen = sel_scr[hh, pl.ds(j, 1), :] > 0.0
            s = jnp.where(chosen, _dot_nt(k, qm_scr[hh]), NEG)
            _softmax_step(hh, s, vt, m_scr, l_scr, acc_scr)
        return 0

    lax.fori_loop(0, qt, past_tile, 0)
    k = k_ref[pl.ds(qt * t, t), :]
    vt = vt_ref[qt]
    kidx, qidx = _key_query_index(t, t, 0, 0)
    for hh in range(2):
        s = jnp.where(kidx <= qidx, _dot_nt(k, qm_scr[hh]), NEG)
        _softmax_step(hh, s, vt, m_scr, l_scr, acc_scr)
    o_ref[...] = _pair_output(0, l_scr, acc_scr)


def _moba_prompt(q, kmean, k16, vt16, batch, seq):
    t = MOBA_T
    nq = seq // t
    nb = seq // MOBA_BLOCK
    npair = N_HEADS_B // 2
    return pl.pallas_call(
        functools.partial(_moba_prompt_kernel, t=t),
        grid=(batch, npair, nq),
        in_specs=[
            pl.BlockSpec((t, LANES), lambda b, p, i: (b * nq + i, p)),
            pl.BlockSpec((None, nb, LANES), lambda b, p, i: (b, 0, p)),
            pl.BlockSpec((seq, LANES), lambda b, p, i: (b, p)),
            pl.BlockSpec((None, None, nq, LANES, t), lambda b, p, i: (b, p, 0, 0, 0)),
        ],
        out_specs=pl.BlockSpec((t, LANES), lambda b, p, i: (b * nq + i, p)),
        out_shape=jax.ShapeDtypeStruct((batch * seq, HB), BF16),
        scratch_shapes=[
            pltpu.VMEM((2, nb, t), F32),
            pltpu.VMEM((2, t, LANES), BF16),
            pltpu.VMEM((2, 1, t), F32),
            pltpu.VMEM((2, 1, t), F32),
            pltpu.VMEM((2, LANES, t), F32),
        ],
        compiler_params=_cparams("parallel", "parallel", "parallel"),
        name="moba_prompt",
    )(q, kmean, k16, vt16)


def _fox_prompt_kernel(q_ref, cq_ref, ck_ref, k_ref, vt_ref, o_ref,
                       qm_scr, m_scr, l_scr, acc_scr, *, tq, tk):
    qt = pl.program_id(2)
    q0 = qt * tq
    _store_pair_queries(q_ref[...], qm_scr, 0)
    _softmax_init(m_scr, l_scr, acc_scr)
    cq = cq_ref[...]

    def tile(j, diagonal):
        k = k_ref[pl.ds(j * tk, tk), :]
        vt = vt_ref[j]
        ck = ck_ref[pl.ds(j * tk, tk), :]
        if diagonal:
            kidx, qidx = _key_query_index(tk, tq, j * tk, q0)
        logits = [_dot_nt(k, qm_scr[hh]) for hh in range(2)]
        for hh in range(2):
            s = logits[hh] + (cq[hh:hh + 1, :] - ck[:, hh:hh + 1])
            if diagonal:
                s = jnp.where(kidx <= qidx, s, NEG)
            _softmax_step(hh, s, vt, m_scr, l_scr, acc_scr)

    n_full = q0 // tk
    lax.fori_loop(0, n_full, lambda j, _: (tile(j, False), 0)[1], 0)
    tile(n_full, True)
    o_ref[...] = _pair_output(0, l_scr, acc_scr)


def _fox_prompt(q, cum_q, cum_k, k16, vt16, batch, seq, tq, tk):
    nq = seq // tq
    nk = seq // tk
    npair = N_HEADS_C // 2
    return pl.pallas_call(
        functools.partial(_fox_prompt_kernel, tq=tq, tk=tk),
        grid=(batch, npair, nq),
        in_specs=[
            pl.BlockSpec((tq, LANES), lambda b, p, i: (b * nq + i, p)),
            pl.BlockSpec((None, None, 2, tq), lambda b, p, i: (b, p, 0, i)),
            pl.BlockSpec((None, None, seq, 2), lambda b, p, i: (b, p, 0, 0)),
            pl.BlockSpec((seq, LANES), lambda b, p, i: (b, p)),
            pl.BlockSpec((None, None, nk, LANES, tk), lambda b, p, i: (b, p, 0, 0, 0)),
        ],
        out_specs=pl.BlockSpec((tq, LANES), lambda b, p, i: (b * nq + i, p)),
        out_shape=jax.ShapeDtypeStruct((batch * seq, HC), BF16),
        scratch_shapes=[
            pltpu.VMEM((2, tq, LANES), BF16),
            pltpu.VMEM((2, 1, tq), F32),
            pltpu.VMEM((2, 1, tq), F32),
            pltpu.VMEM((2, LANES, tq), F32),
        ],
        compiler_params=_cparams("parallel", "parallel", "parallel"),
        name="fox_prompt",
    )(q, cum_q, cum_k, k16, vt16)


def _router_kernel(x_ref, rwh_ref, rwl_ref, rb_ref, g_ref):
    x_hi, x_lo = _split2(x_ref[...])
    logits = _dot3(rwh_ref[...], rwl_ref[...], x_hi, x_lo, _dot_nt)
    scores = 1.0 / (1.0 + jnp.exp(-logits))
    biased = scores + rb_ref[...]
    rows = [biased[e:e + 1, :] for e in range(N_EXPERTS)]
    ninf = jnp.full_like(rows[0], -jnp.inf)

    def top2(vals):
        mx = functools.reduce(jnp.maximum, vals)
        picks1, found = [], jnp.zeros_like(mx) > 0
        for vv in vals:
            p = (vv == mx) & ~found
            found = found | p
            picks1.append(p)
        rest = [jnp.where(p, ninf, vv) for p, vv in zip(picks1, vals)]
        mx2 = functools.reduce(jnp.maximum, rest)
        picks2, found = [], jnp.zeros_like(mx) > 0
        for vv in rest:
            p = (vv == mx2) & ~found
            found = found | p
            picks2.append(p)
        return mx, mx2, picks1, picks2

    grp_score = []
    for gi in range(N_GROUPS):
        m1, m2, _, _ = top2(rows[gi * EXPERTS_PER_GROUP:(gi + 1) * EXPERTS_PER_GROUP])
        grp_score.append(m1 + m2)
    best = grp_score[0]
    g_sel = jnp.zeros_like(best, dtype=I32)
    for gi in range(1, N_GROUPS):
        better = grp_score[gi] > best
        best = jnp.where(better, grp_score[gi], best)
        g_sel = jnp.where(better, gi, g_sel)
    masked = [jnp.where(g_sel == e // EXPERTS_PER_GROUP, rows[e], ninf) for e in range(N_EXPERTS)]
    _, _, p1, p2 = top2(masked)
    zero = jnp.zeros_like(best)
    w1 = functools.reduce(jnp.add, [jnp.where(p1[e], scores[e:e + 1, :], zero) for e in range(N_EXPERTS)])
    w2 = functools.reduce(jnp.add, [jnp.where(p2[e], scores[e:e + 1, :], zero) for e in range(N_EXPERTS)])
    tot = w1 + w2
    for e in range(N_EXPERTS):
        g_ref[e:e + 1, :] = jnp.where(p1[e], w1 / tot, zero) + jnp.where(p2[e], w2 / tot, zero)


def _router(x2d, rwt_hi, rwt_lo, rb, tm):
    n = x2d.shape[0]
    full = lambda a: pl.BlockSpec(a.shape, lambda i: (0, 0))
    return pl.pallas_call(
        _router_kernel,
        grid=(n // tm,),
        in_specs=[pl.BlockSpec((tm, D_MODEL), lambda i: (i, 0)), full(rwt_hi), full(rwt_lo), full(rb)],
        out_specs=pl.BlockSpec((N_EXPERTS, tm), lambda i: (0, i)),
        out_shape=jax.ShapeDtypeStruct((N_EXPERTS, n), F32),
        compiler_params=_cparams("parallel"),
        name="moe_router",
    )(x2d, rwt_hi, rwt_lo, rb)


def _moe_kernel(x_ref, gates_ref, wg_ref, wu_ref, wd_ref, g_ref, b_ref, y_ref, xb_scr, acc_scr):
    e = pl.program_id(1)

    @pl.when(e == 0)
    def _():
        xb_scr[...] = x_ref[...].astype(BF16)
        acc_scr[...] = jnp.zeros_like(acc_scr)

    xb = xb_scr[...]
    gate = _dot(xb, wg_ref[0])
    up = _dot(xb, wu_ref[0])
    h = (gate * (1.0 / (1.0 + jnp.exp(-gate))) * up).astype(BF16)
    down = _dot(h, wd_ref[0])
    gates = gates_ref[...]
    lane = lax.broadcasted_iota(I32, gates.shape, 1)
    w = jnp.sum(jnp.where(lane == e, gates, 0.0), axis=1, keepdims=True)
    acc_scr[...] += w * down

    @pl.when(e == pl.num_programs(1) - 1)
    def _():
        y_ref[...] = _layer_norm(ALPHA * x_ref[...] + acc_scr[...], g_ref[...], b_ref[...])


def _moe_ln(x2d, gates, wg16, wu16, wd16, g, b, tm):
    n = x2d.shape[0]
    full = lambda a: pl.BlockSpec(a.shape, lambda i, e: (0, 0))
    return pl.pallas_call(
        _moe_kernel,
        grid=(n // tm, N_EXPERTS),
        in_specs=[
            pl.BlockSpec((tm, D_MODEL), lambda i, e: (i, 0)),
            pl.BlockSpec((tm, N_EXPERTS), lambda i, e: (i, 0)),
            pl.BlockSpec((1, D_MODEL, D_EXPERT), lambda i, e: (e, 0, 0)),
            pl.BlockSpec((1, D_MODEL, D_EXPERT), lambda i, e: (e, 0, 0)),
            pl.BlockSpec((1, D_EXPERT, D_MODEL), lambda i, e: (e, 0, 0)),
            full(g), full(b),
        ],
        out_specs=pl.BlockSpec((tm, D_MODEL), lambda i, e: (i, 0)),
        out_shape=jax.ShapeDtypeStruct((n, D_MODEL), F32),
        scratch_shapes=[pltpu.VMEM((tm, D_MODEL), BF16), pltpu.VMEM((tm, D_MODEL), F32)],
        compiler_params=_cparams("parallel", "arbitrary"),
        name="moe_experts_ln",
    )(x2d, gates, wg16, wu16, wd16, g, b)


def _dsa_sample_score_kernel(pt_ref, q_ref, w_ref, knew_ref, *rest, n_pages):
    page_refs = rest[:n_pages]
    s_ref = rest[n_pages]
    q = q_ref[...]
    q_hi, q_lo = _split2(q)
    w = w_ref[...]
    for p in range(n_pages):
        k_hi, k_lo = _split2(page_refs[p][...])
        d = _dot3(q_hi, q_lo, k_hi, k_lo, _dot_nt)
        s_ref[p:p + 1, :] = jnp.sum(w * jnp.maximum(d, 0.0), axis=0, keepdims=True) * IDX_SCALE
    k_new = knew_ref[...][:, 0:IDX_DIM]
    d_new = jnp.sum(q * k_new, axis=1, keepdims=True)
    s_new = jnp.sum(w * jnp.maximum(d_new, 0.0), axis=0, keepdims=True) * IDX_SCALE
    lane = lax.broadcasted_iota(I32, (1, PAGE_SIZE), 1)
    s_ref[n_pages:n_pages + 1, :] = jnp.where(lane == 0, s_new, -jnp.inf)


def _dsa_sample_scores(page_table, q_i, w_i, tail, cache_idx):
    nseq, n_pages = page_table.shape
    page = lambda p: pl.BlockSpec((None, PAGE_SIZE, IDX_DIM), lambda b, pt: (pt[b, p], 0, 0))
    grid_spec = pltpu.PrefetchScalarGridSpec(
        num_scalar_prefetch=1,
        grid=(nseq,),
        in_specs=[
            pl.BlockSpec((None, IDX_HEADS, IDX_DIM), lambda b, pt: (b, 0, 0)),
            pl.BlockSpec((None, IDX_HEADS, 1), lambda b, pt: (b, 0, 0)),
            pl.BlockSpec((None, 1, LANES), lambda b, pt: (b, 0, 0)),
        ] + [page(p) for p in range(n_pages)],
        out_specs=pl.BlockSpec((None, n_pages + 1, PAGE_SIZE), lambda b, pt: (b, 0, 0)),
    )
    return pl.pallas_call(
        functools.partial(_dsa_sample_score_kernel, n_pages=n_pages),
        grid_spec=grid_spec,
        out_shape=jax.ShapeDtypeStruct((nseq, n_pages + 1, PAGE_SIZE), F32),
        compiler_params=_cparams("parallel"),
        name="dsa_sample_scores",
    )(page_table, q_i, w_i, tail, *([cache_idx] * n_pages))


def _select_kernel(s_ref, valid_ref, bias_ref, key_scr, *, topk, key_bits):
    n_tiles, tk, nseq = s_ref.shape
    for j in range(n_tiles):
        key_scr[j] = jnp.where(valid_ref[j] > 0.0, _sortable_key(s_ref[j]), INT_MIN)
    t, c = _topk_threshold(key_scr, n_tiles, topk, key_bits)
    for j in range(n_tiles):
        kt = key_scr[j]
        kidx = j * tk + lax.broadcasted_iota(I32, kt.shape, 0)
        bias_ref[j] = jnp.where((kt > t) | ((kt == t) & (kidx <= c)), 0.0, NEG)


def _select_topk_bias(scores_t, valid_t, topk):
    n_chunks = scores_t.shape[0]
    key_bits = max(1, (n_chunks * PAGE_SIZE - 1).bit_length())
    return pl.pallas_call(
        functools.partial(_select_kernel, topk=topk, key_bits=key_bits),
        out_shape=jax.ShapeDtypeStruct(scores_t.shape, F32),
        scratch_shapes=[pltpu.VMEM(scores_t.shape, I32)],
        compiler_params=pltpu.CompilerParams(vmem_limit_bytes=VMEM_LIMIT),
        name="dsa_sample_select",
    )(scores_t, valid_t)


def _own_head(n_heads, n_rows):
    r = lax.broadcasted_iota(I32, (n_heads, n_rows), 0)
    col = lax.broadcasted_iota(I32, (n_heads, n_rows), 1)
    return col % n_heads == r


def _decode_logits(q_ref, k_ref):
    n_heads = q_ref.shape[0]
    qs = (q_ref[...] * QK_SCALE).astype(BF16)
    s = _dot_nt(qs, k_ref[...].astype(BF16))
    return s, _own_head(n_heads, k_ref.shape[0])


def _decode_init(m_scr, l_scr, acc_scr):
    m_scr[...] = jnp.full(m_scr.shape, NEG, F32)
    l_scr[...] = jnp.zeros(l_scr.shape, F32)
    acc_scr[...] = jnp.zeros(acc_scr.shape, F32)


def _decode_step(s, v, m_scr, l_scr, acc_scr):
    m_old = m_scr[...]
    m_new = jnp.maximum(m_old, jnp.max(s, axis=1, keepdims=True))
    alpha = jnp.exp(m_old - m_new)
    p = jnp.exp(s - m_new)
    l_scr[...] = alpha * l_scr[...] + jnp.sum(p, axis=1, keepdims=True)
    acc_scr[...] = alpha * acc_scr[...] + _dot(p.astype(BF16), v)
    m_scr[...] = m_new


def _decode_finish(q, bias_new, k_new, v_new, m_scr, l_scr, acc_scr):
    s_new = jnp.sum((q * QK_SCALE) * k_new, axis=1, keepdims=True) + bias_new
    m_old = m_scr[...]
    m_new = jnp.maximum(m_old, s_new)
    alpha = jnp.exp(m_old - m_new)
    p_new = jnp.exp(s_new - m_new)
    l = alpha * l_scr[...] + p_new
    acc = alpha * acc_scr[...] + p_new * v_new
    return acc / l


def _dsa_sample_attn_kernel(pt_ref, q_ref, bias_ref, knew_ref, vnew_ref, k_ref, v_ref, o_ref,
                            m_scr, l_scr, acc_scr):
    p = pl.program_id(1)

    @pl.when(p == 0)
    def _():
        _decode_init(m_scr, l_scr, acc_scr)

    s, own = _decode_logits(q_ref, k_ref)
    s = jnp.where(own, s + bias_ref[pl.ds(p, 1), :], NEG)
    _decode_step(s, v_ref[...].astype(BF16), m_scr, l_scr, acc_scr)

    @pl.when(p == pl.num_programs(1) - 1)
    def _():
        b_new = bias_ref[pl.ds(p + 1, 1), :][:, 0:1]
        o_ref[...] = _decode_finish(q_ref[...], b_new, knew_ref[...], vnew_ref[...],
                                    m_scr, l_scr, acc_scr).astype(BF16)


def _dsa_sample_attn(page_table, q, bias, k_new, v_new, cache_k, cache_v):
    nseq, n_pages = page_table.shape
    rows = PAGE_SIZE * N_HEADS_A
    hd = pl.BlockSpec((None, N_HEADS_A, HEAD_DIM), lambda b, p, pt: (b, 0, 0))
    page = pl.BlockSpec((None, rows, HEAD_DIM), lambda b, p, pt: (pt[b, p], 0, 0))
    grid_spec = pltpu.PrefetchScalarGridSpec(
        num_scalar_prefetch=1,
        grid=(nseq, n_pages),
        in_specs=[hd, pl.BlockSpec((None, n_pages + 1, rows), lambda b, p, pt: (b, 0, 0)),
                  hd, hd, page, page],
        out_specs=hd,
        scratch_shapes=[pltpu.VMEM((N_HEADS_A, 1), F32), pltpu.VMEM((N_HEADS_A, 1), F32),
                        pltpu.VMEM((N_HEADS_A, HEAD_DIM), F32)],
    )
    return pl.pallas_call(
        _dsa_sample_attn_kernel,
        grid_spec=grid_spec,
        out_shape=jax.ShapeDtypeStruct((nseq, N_HEADS_A, HEAD_DIM), BF16),
        compiler_params=_cparams("parallel", "arbitrary"),
        name="dsa_sample_attn",
    )(page_table, q, bias, k_new, v_new, cache_k, cache_v)


def _moba_sample_kernel(pt_ref, q_ref, knew_ref, vnew_ref, k_ref, v_ref, o_ref,
                        ksum_scr, gate_scr, mb_scr, lb_scr, accb_scr,
                        m_scr, l_scr, acc_scr, *, pages_per_block):
    p = pl.program_id(1)
    n_pages = pl.num_programs(1)
    n_blocks = gate_scr.shape[0]

    @pl.when(p % pages_per_block == 0)
    def _():
        ksum_scr[...] = jnp.zeros(ksum_scr.shape, F32)
        _decode_init(m_scr, l_scr, acc_scr)

    part = k_ref[0:N_HEADS_B, :]
    for i in range(1, PAGE_SIZE):
        part = part + k_ref[i * N_HEADS_B:(i + 1) * N_HEADS_B, :]
    ksum_scr[...] += part
    s, own = _decode_logits(q_ref, k_ref)
    _decode_step(jnp.where(own, s, NEG), v_ref[...].astype(BF16), m_scr, l_scr, acc_scr)

    @pl.when(p % pages_per_block == pages_per_block - 1)
    def _():
        n = p // pages_per_block
        k_mean = ksum_scr[...] / float(pages_per_block * PAGE_SIZE)
        gate_scr[n] = jnp.sum(q_ref[...] * k_mean, axis=1, keepdims=True)
        mb_scr[n] = m_scr[...]
        lb_scr[n] = l_scr[...]
        accb_scr[n] = acc_scr[...]

    @pl.when(p == n_pages - 1)
    def _():
        gates = [gate_scr[n] for n in range(n_blocks)]
        chosen = [jnp.zeros_like(gates[0]) > 0 for _ in range(n_blocks)]
        for _ in range(min(MOBA_TOPK, n_blocks + 1)):
            mx = functools.reduce(jnp.maximum, gates)
            found = jnp.zeros_like(mx) > 0
            for n in range(n_blocks):
                pick = (gates[n] == mx) & ~found & (mx > -jnp.inf)
                found = found | pick
                chosen[n] = chosen[n] | pick
                gates[n] = jnp.where(pick, -jnp.inf, gates[n])
        s_new = jnp.sum((q_ref[...] * QK_SCALE) * knew_ref[...], axis=1, keepdims=True)
        m_tot = s_new
        for n in range(n_blocks):
            m_tot = jnp.maximum(m_tot, jnp.where(chosen[n], mb_scr[n], NEG))
        p_new = jnp.exp(s_new - m_tot)
        l = p_new
        acc = p_new * vnew_ref[...]
        for n in range(n_blocks):
            wgt = jnp.where(chosen[n], jnp.exp(mb_scr[n] - m_tot), 0.0)
            l = l + wgt * lb_scr[n]
            acc = acc + wgt * accb_scr[n]
        o_ref[...] = (acc / l).astype(BF16)


def _moba_sample(page_table, q, k_new, v_new, cache_k, cache_v):
    nseq, n_pages = page_table.shape
    ppb = MOBA_BLOCK // PAGE_SIZE
    n_blocks = n_pages // ppb
    rows = PAGE_SIZE * N_HEADS_B
    hd = pl.BlockSpec((None, N_HEADS_B, HEAD_DIM), lambda b, p, pt: (b, 0, 0))
    page = pl.BlockSpec((None, rows, HEAD_DIM), lambda b, p, pt: (pt[b, p], 0, 0))
    stat = lambda *lead: pltpu.VMEM((*lead, N_HEADS_B, 1), F32)
    vec = lambda *lead: pltpu.VMEM((*lead, N_HEADS_B, HEAD_DIM), F32)
    grid_spec = pltpu.PrefetchScalarGridSpec(
        num_scalar_prefetch=1,
        grid=(nseq, n_pages),
        in_specs=[hd, hd, hd, page, page],
        out_specs=hd,
        scratch_shapes=[vec(), stat(n_blocks), stat(n_blocks), stat(n_blocks), vec(n_blocks),
                        stat(), stat(), vec()],
    )
    return pl.pallas_call(
        functools.partial(_moba_sample_kernel, pages_per_block=ppb),
        grid_spec=grid_spec,
        out_shape=jax.ShapeDtypeStruct((nseq, N_HEADS_B, HEAD_DIM), BF16),
        compiler_params=_cparams("parallel", "arbitrary"),
        name="moba_sample",
    )(page_table, q, k_new, v_new, cache_k, cache_v)


def _fox_sample_bias_kernel(pt_ref, lnew_ref, *rest, n_pages):
    page_refs = rest[:n_pages]
    bias_ref, bnew_ref = rest[n_pages], rest[n_pages + 1]
    r = lax.broadcasted_iota(I32, (PAGE_SIZE, PAGE_SIZE), 0)
    cc = lax.broadcasted_iota(I32, (PAGE_SIZE, PAGE_SIZE), 1)
    tril = jnp.where(cc <= r, 1.0, 0.0).astype(BF16)
    carry = jnp.zeros((1, N_HEADS_C), F32)
    cums = []
    for p in range(n_pages):
        hi, mid, lo = _split3(page_refs[p][...])
        cum = (_dot(tril, hi) + _dot(tril, mid) + _dot(tril, lo)) + carry
        cums.append(cum)
        carry = cum[PAGE_SIZE - 1:PAGE_SIZE, :]
    cum_q = carry + lnew_ref[...][:, 0:N_HEADS_C]
    for p in range(n_pages):
        bias_ref[p * PAGE_SIZE:(p + 1) * PAGE_SIZE, :] = cum_q - cums[p]
    bnew_ref[...] = cum_q - cum_q


def _fox_sample_bias(page_table, logf_new, cache_logf):
    nseq, n_pages = page_table.shape
    page = lambda p: pl.BlockSpec((None, PAGE_SIZE, N_HEADS_C), lambda b, pt: (pt[b, p], 0, 0))
    grid_spec = pltpu.PrefetchScalarGridSpec(
        num_scalar_prefetch=1,
        grid=(nseq,),
        in_specs=[pl.BlockSpec((None, 1, LANES), lambda b, pt: (b, 0, 0))] + [page(p) for p in range(n_pages)],
        out_specs=[pl.BlockSpec((None, n_pages * PAGE_SIZE, N_HEADS_C), lambda b, pt: (b, 0, 0)),
                   pl.BlockSpec((None, 1, N_HEADS_C), lambda b, pt: (b, 0, 0))],
    )
    return pl.pallas_call(
        functools.partial(_fox_sample_bias_kernel, n_pages=n_pages),
        grid_spec=grid_spec,
        out_shape=[jax.ShapeDtypeStruct((nseq, n_pages * PAGE_SIZE, N_HEADS_C), F32),
                   jax.ShapeDtypeStruct((nseq, 1, N_HEADS_C), F32)],
        compiler_params=_cparams("parallel"),
        name="fox_sample_bias",
    )(page_table, logf_new, *([cache_logf] * n_pages))


def _fox_sample_attn_kernel(pt_ref, q_ref, bias_ref, bnew_ref, knew_ref, vnew_ref, k_ref, v_ref, o_ref,
                            m_scr, l_scr, acc_scr):
    p = pl.program_id(1)

    @pl.when(p == 0)
    def _():
        _decode_init(m_scr, l_scr, acc_scr)

    s, own = _decode_logits(q_ref, k_ref)
    s = jnp.where(own, s + bias_ref[pl.ds(p, 1), :], NEG)
    _decode_step(s, v_ref[...].astype(BF16), m_scr, l_scr, acc_scr)

    @pl.when(p == pl.num_programs(1) - 1)
    def _():
        o_ref[...] = _decode_finish(q_ref[...], bnew_ref[...], knew_ref[...], vnew_ref[...],
                                    m_scr, l_scr, acc_scr).astype(BF16)


def _fox_sample_attn(page_table, q, bias, bias_new, k_new, v_new, cache_k, cache_v):
    nseq, n_pages = page_table.shape
    rows = PAGE_SIZE * N_HEADS_C
    hd = pl.BlockSpec((None, N_HEADS_C, HEAD_DIM), lambda b, p, pt: (b, 0, 0))
    page = pl.BlockSpec((None, rows, HEAD_DIM), lambda b, p, pt: (pt[b, p], 0, 0))
    grid_spec = pltpu.PrefetchScalarGridSpec(
        num_scalar_prefetch=1,
        grid=(nseq, n_pages),
        in_specs=[hd, pl.BlockSpec((None, n_pages, rows), lambda b, p, pt: (b, 0, 0)),
                  pl.BlockSpec((None, N_HEADS_C, 1), lambda b, p, pt: (b, 0, 0)),
                  hd, hd, page, page],
        out_specs=hd,
        scratch_shapes=[pltpu.VMEM((N_HEADS_C, 1), F32), pltpu.VMEM((N_HEADS_C, 1), F32),
                        pltpu.VMEM((N_HEADS_C, HEAD_DIM), F32)],
    )
    return pl.pallas_call(
        _fox_sample_attn_kernel,
        grid_spec=grid_spec,
        out_shape=jax.ShapeDtypeStruct((nseq, N_HEADS_C, HEAD_DIM), BF16),
        compiler_params=_cparams("parallel", "arbitrary"),
        name="fox_sample_attn",
    )(page_table, q, bias, bias_new, k_new, v_new, cache_k, cache_v)


def _pad_cols(w, width):
    return jnp.pad(w, ((0, 0), (0, width - w.shape[1])))


def _rope_inv_freq():
    half = HEAD_DIM // 8
    inv = ROPE_THETA ** (-jnp.arange(half, dtype=F32) / half)
    per_head = jnp.concatenate([inv, inv, jnp.zeros((HEAD_DIM - 2 * half,), F32)])
    return jnp.tile(per_head, LANES // HEAD_DIM)[None, :]


def _transposed_values(v16, batch, seq, tk):
    npair = v16.shape[1] // LANES
    return v16.reshape(batch, seq // tk, tk, npair, LANES).transpose(0, 3, 1, 4, 2)


def _ab_layer_prompt(x2d, w16, w_out16, invf, g, b, batch, seq):
    pos = jnp.tile(jnp.arange(seq, dtype=F32), batch)[:, None]
    qa, ka, va, qb, kb, vb, qi, tail, ka16, va16, kb16, vb16 = _ab_project(x2d, w16, pos, invf, PROJ_TM)
    k_idx = tail[:, :IDX_DIM]
    ki_hi, ki_lo = _split2(k_idx)
    wt = tail[:, IDX_DIM:IDX_DIM + IDX_HEADS].reshape(batch, seq, IDX_HEADS).transpose(0, 2, 1)
    o_a = _dsa_prompt(qa, qi, wt, ki_hi, ki_lo, ka16, _transposed_values(va16, batch, seq, DSA_TK),
                      batch, seq, DSA_TQ, DSA_TK)
    kmean = _block_means(kb).reshape(batch, seq // MOBA_BLOCK, HB)
    o_b = _moba_prompt(qb, kmean, kb16, _transposed_values(vb16, batch, seq, MOBA_T), batch, seq)
    o = jnp.concatenate([o_a, o_b], axis=1)
    y = _out_proj_ln(o, w_out16, x2d, g, b, PROJ_TM)
    return y, (ka, va, k_idx, kb, vb)


def _ab_layer_sample(x2d, w16, w_out16, invf, g, b, past_len, page_table,
                     cache_a_k, cache_a_v, cache_a_idx, cache_b_k, cache_b_v):
    nseq = x2d.shape[0]
    n_pages = page_table.shape[1]
    pos = jnp.full((nseq, 1), past_len, F32)
    qa, ka, va, qb, kb, vb, qi, tail, _, _, _, _ = _ab_project(x2d, w16, pos, invf, nseq)
    k_idx = tail[:, :IDX_DIM]
    w_i = tail[:, IDX_DIM:IDX_DIM + IDX_HEADS]
    heads = lambda a: a.reshape(nseq, -1, HEAD_DIM)
    rows = lambda c: c.reshape(c.shape[0], -1, HEAD_DIM)
    scores = _dsa_sample_scores(page_table, heads(qi), w_i.reshape(nseq, IDX_HEADS, 1),
                                tail.reshape(nseq, 1, LANES), cache_a_idx)
    n_keys = n_pages * PAGE_SIZE + 1
    valid = (jnp.arange((n_pages + 1) * PAGE_SIZE) < n_keys).astype(F32)
    valid_t = jnp.broadcast_to(valid.reshape(n_pages + 1, PAGE_SIZE, 1), (n_pages + 1, PAGE_SIZE, nseq))
    bias_t = _select_topk_bias(scores.transpose(1, 2, 0), valid_t, min(DSA_TOPK, n_keys // 4))
    bias = jnp.repeat(bias_t.transpose(2, 0, 1), N_HEADS_A, axis=2)
    o_a = _dsa_sample_attn(page_table, heads(qa), bias, heads(ka), heads(va), rows(cache_a_k), rows(cache_a_v))
    o_b = _moba_sample(page_table, heads(qb), heads(kb), heads(vb), rows(cache_b_k), rows(cache_b_v))
    o = jnp.concatenate([o_a.reshape(nseq, HA), o_b.reshape(nseq, HB)], axis=1)
    y = _out_proj_ln(o, w_out16, x2d, g, b, nseq)
    return y, (ka, va, k_idx, kb, vb)


def _fox_layer_prompt(x2d, w16, bf_pad, w_out16, g, b, batch, seq):
    q, k, v, logf, cum, k16, v16 = _fox_project(x2d, w16, bf_pad, PROJ_TM, seq)
    npair = N_HEADS_C // 2
    cum_k = cum[:, :N_HEADS_C].reshape(batch, seq, npair, 2).transpose(0, 2, 1, 3)
    cum_q = cum_k.transpose(0, 1, 3, 2)
    o = _fox_prompt(q, cum_q, cum_k, k16, _transposed_values(v16, batch, seq, FOX_TK),
                    batch, seq, FOX_TQ, FOX_TK)
    y = _out_proj_ln(o, w_out16, x2d, g, b, PROJ_TM)
    return y, (k, v, logf[:, :N_HEADS_C])


def _fox_layer_sample(x2d, w16, bf_pad, w_out16, g, b, page_table, cache_c_k, cache_c_v, cache_c_logf):
    nseq, n_pages = page_table.shape
    q, k, v, logf, _, _, _ = _fox_project(x2d, w16, bf_pad, nseq, nseq)
    heads = lambda a: a.reshape(nseq, N_HEADS_C, HEAD_DIM)
    rows = lambda c: c.reshape(c.shape[0], -1, HEAD_DIM)
    bias, bias_new = _fox_sample_bias(page_table, logf.reshape(nseq, 1, LANES), cache_c_logf)
    o = _fox_sample_attn(page_table, heads(q), bias.reshape(nseq, n_pages, PAGE_SIZE * N_HEADS_C),
                         bias_new.transpose(0, 2, 1), heads(k), heads(v), rows(cache_c_k), rows(cache_c_v))
    y = _out_proj_ln(o.reshape(nseq, HC), w_out16, x2d, g, b, nseq)
    return y, (k, v, logf[:, :N_HEADS_C])


def _moe_layer(x2d, rwt_hi, rwt_lo, rb, wg16, wu16, wd16, g, b, tm):
    gates_t = _router(x2d, rwt_hi, rwt_lo, rb, tm)
    return _moe_ln(x2d, gates_t.T, wg16, wu16, wd16, g, b, tm)


def kernel(x_prompt, x_sample, cache_a_k, cache_a_v, cache_a_idx, cache_b_k, cache_b_v, cache_c_k, cache_c_v, cache_c_logf, page_table, w_in_ab, w_out_ab, w_in_fox, b_forget, w_out_fox, ln_mix_g, ln_mix_b, ln_ffn_g, ln_ffn_b, router_w, router_bias, exp_w_gate, exp_w_up, exp_w_down):
    batch, seq, _ = x_prompt.shape
    nseq = x_sample.shape[0]
    past_len = page_table.shape[1] * PAGE_SIZE

    w_ab16 = _pad_cols(w_in_ab, 7 * HA + LANES).astype(BF16)
    w_out_ab16 = w_out_ab.astype(BF16)
    w_fox16 = _pad_cols(w_in_fox, 3 * HC + LANES).astype(BF16)
    w_out_fox16 = w_out_fox.astype(BF16)
    bf_pad = jnp.pad(b_forget, (0, LANES - N_HEADS_C))[None, :]
    invf = _rope_inv_freq()
    rwt_hi, rwt_lo = _split2(router_w.T)
    rb = router_bias[:, None]
    wg16, wu16, wd16 = exp_w_gate.astype(BF16), exp_w_up.astype(BF16), exp_w_down.astype(BF16)
    row = lambda a, i: a[i][None, :]

    xp = x_prompt.reshape(batch * seq, D_MODEL)
    xs = x_sample.reshape(nseq, D_MODEL)

    xp, (pa_k, pa_v, pa_idx, pb_k, pb_v) = _ab_layer_prompt(
        xp, w_ab16, w_out_ab16, invf, row(ln_mix_g, 0), row(ln_mix_b, 0), batch, seq)
    xs, (sa_k, sa_v, sa_idx, sb_k, sb_v) = _ab_layer_sample(
        xs, w_ab16, w_out_ab16, invf, row(ln_mix_g, 0), row(ln_mix_b, 0), past_len, page_table,
        cache_a_k, cache_a_v, cache_a_idx, cache_b_k, cache_b_v)
    xp = _moe_layer(xp, rwt_hi, rwt_lo, rb, wg16[0], wu16[0], wd16[0], row(ln_ffn_g, 0), row(ln_ffn_b, 0), MOE_TM)
    xs = _moe_layer(xs, rwt_hi, rwt_lo, rb, wg16[0], wu16[0], wd16[0], row(ln_ffn_g, 0), row(ln_ffn_b, 0), nseq)

    xp, (pc_k, pc_v, pc_logf) = _fox_layer_prompt(
        xp, w_fox16, bf_pad, w_out_fox16, row(ln_mix_g, 1), row(ln_mix_b, 1), batch, seq)
    xs, (sc_k, sc_v, sc_logf) = _fox_layer_sample(
        xs, w_fox16, bf_pad, w_out_fox16, row(ln_mix_g, 1), row(ln_mix_b, 1), page_table,
        cache_c_k, cache_c_v, cache_c_logf)
    xp = _moe_layer(xp, rwt_hi, rwt_lo, rb, wg16[1], wu16[1], wd16[1], row(ln_ffn_g, 1), row(ln_ffn_b, 1), MOE_TM)
    xs = _moe_layer(xs, rwt_hi, rwt_lo, rb, wg16[1], wu16[1], wd16[1], row(ln_ffn_g, 1), row(ln_ffn_b, 1), nseq)

    hd = lambda a, nh, lead: a.reshape(*lead, nh, HEAD_DIM)
    lp, ls = (batch, seq), (nseq, 1)
    return (xp.reshape(batch, seq, D_MODEL), xs.reshape(nseq, 1, D_MODEL),
            hd(pa_k, N_HEADS_A, lp), hd(pa_v, N_HEADS_A, lp), pa_idx.reshape(batch, seq, IDX_DIM),
            hd(pb_k, N_HEADS_B, lp), hd(pb_v, N_HEADS_B, lp),
            hd(pc_k, N_HEADS_C, lp), hd(pc_v, N_HEADS_C, lp), pc_logf.reshape(batch, seq, N_HEADS_C),
            hd(sa_k, N_HEADS_A, ls), hd(sa_v, N_HEADS_A, ls), sa_idx.reshape(nseq, 1, IDX_DIM),
            hd(sb_k, N_HEADS_B, ls), hd(sb_v, N_HEADS_B, ls),
            hd(sc_k, N_HEADS_C, ls), hd(sc_v, N_HEADS_C, ls), sc_logf.reshape(nseq, 1, N_HEADS_C))
```

```python
import functools

import jax
import jax.numpy as jnp
from jax import lax
from jax.experimental import pallas as pl
from jax.experimental.pallas import tpu as pltpu

F32 = jnp.float32
BF16 = jnp.bfloat16
I32 = jnp.int32

D_MODEL = 1024
DEPTH = 2
PAGE_SIZE = 128
HEAD_DIM = 64
N_HEADS_A = 8
N_HEADS_B = 8
N_HEADS_C = 16
IDX_HEADS = 8
IDX_DIM = 64
DSA_TOPK = 256
MOBA_BLOCK = 256
MOBA_TOPK = 3
ROPE_THETA = 500000.0
N_EXPERTS = 16
N_GROUPS = 4
EXPERTS_PER_GROUP = N_EXPERTS // N_GROUPS
D_EXPERT = 512
ALPHA = (2 * DEPTH) ** 0.25
LN_EPS = 1e-5
HA = N_HEADS_A * HEAD_DIM
HB = N_HEADS_B * HEAD_DIM
HC = N_HEADS_C * HEAD_DIM
QK_SCALE = HEAD_DIM ** -0.5
IDX_SCALE = IDX_DIM ** -0.5

LANES = 128
NEG = -1e30
INT_MIN = -2 ** 31
VMEM_LIMIT = 56 * 2 ** 20

PROJ_TM = 256
DSA_TQ = 256
DSA_TK = 512
MOBA_T = MOBA_BLOCK
FOX_TQ = 256
FOX_TK = 1024
MOE_TM = 512


def _cparams(*sem):
    return pltpu.CompilerParams(dimension_semantics=sem, vmem_limit_bytes=VMEM_LIMIT)


def _dot(a, b):
    return jnp.dot(a, b, preferred_element_type=F32)


def _dot_nt(a, b):
    return lax.dot_general(a, b, (((1,), (1,)), ((), ())), preferred_element_type=F32)


def _split2(x):
    hi = x.astype(BF16)
    lo = (x - hi.astype(F32)).astype(BF16)
    return hi, lo


def _split3(x):
    hi = x.astype(BF16)
    r = x - hi.astype(F32)
    mid = r.astype(BF16)
    lo = (r - mid.astype(F32)).astype(BF16)
    return hi, mid, lo


def _dot3(a_hi, a_lo, b_hi, b_lo, dot):
    return dot(a_hi, b_hi) + (dot(a_hi, b_lo) + dot(a_lo, b_hi))


def _layer_norm(z, g, b):
    mu = jnp.mean(z, axis=-1, keepdims=True)
    d = z - mu
    var = jnp.mean(d * d, axis=-1, keepdims=True)
    return d * lax.rsqrt(var + LN_EPS) * g + b


def _log_sigmoid(z):
    return -(jnp.maximum(-z, 0.0) + jnp.log1p(jnp.exp(-jnp.abs(z))))


def _rotary_tables(pos_ref, invf_ref):
    ang = pos_ref[...] * invf_ref[...]
    c = jnp.cos(ang)
    s = jnp.sin(ang)
    f = lax.broadcasted_iota(I32, ang.shape, 1) % HEAD_DIM
    s_up = jnp.where(f < 8, -s, 0.0)
    s_dn = jnp.where(f >= 8, s, 0.0)
    return c, s_up, s_dn


def _rotate(h, c, s_up, s_dn):
    outs = []
    for j in range(h.shape[1] // LANES):
        hc = h[:, j * LANES:(j + 1) * LANES]
        outs.append(hc * c + pltpu.roll(hc, LANES - 8, 1) * s_up + pltpu.roll(hc, 8, 1) * s_dn)
    return outs[0] if len(outs) == 1 else jnp.concatenate(outs, axis=1)


def _ab_proj_kernel(x_ref, w_ref, pos_ref, invf_ref,
                    qa_ref, ka_ref, va_ref, qb_ref, kb_ref, vb_ref, qi_ref, tail_ref,
                    ka16_ref, va16_ref, kb16_ref, vb16_ref):
    xb = x_ref[...].astype(BF16)
    c, s_up, s_dn = _rotary_tables(pos_ref, invf_ref)

    def seg(j, width=HA):
        return _dot(xb, w_ref[:, j * HA:j * HA + width])

    qa_ref[...] = _rotate(seg(0), c, s_up, s_dn)
    ka = _rotate(seg(1), c, s_up, s_dn)
    ka_ref[...] = ka
    ka16_ref[...] = ka.astype(BF16)
    va = seg(2)
    va_ref[...] = va
    va16_ref[...] = va.astype(BF16)
    qb_ref[...] = _rotate(seg(3), c, s_up, s_dn)
    kb = _rotate(seg(4), c, s_up, s_dn)
    kb_ref[...] = kb
    kb16_ref[...] = kb.astype(BF16)
    vb = seg(5)
    vb_ref[...] = vb
    vb16_ref[...] = vb.astype(BF16)
    qi_ref[...] = _rotate(seg(6), c, s_up, s_dn)
    t = seg(7, LANES)
    lane = lax.broadcasted_iota(I32, t.shape, 1)
    is_key = lane < IDX_DIM
    ct = jnp.where(is_key, c, IDX_HEADS ** -0.5)
    tail_ref[...] = (t * ct + pltpu.roll(t, LANES - 8, 1) * jnp.where(is_key, s_up, 0.0)
                     + pltpu.roll(t, 8, 1) * jnp.where(is_key, s_dn, 0.0))


def _ab_project(x2d, w16, pos, invf, tm):
    n = x2d.shape[0]
    wide = jax.ShapeDtypeStruct((n, HA), F32)
    wide16 = jax.ShapeDtypeStruct((n, HA), BF16)
    row = lambda w: pl.BlockSpec((tm, w), lambda i: (i, 0))
    full = lambda a: pl.BlockSpec(a.shape, lambda i: (0, 0))
    return pl.pallas_call(
        _ab_proj_kernel,
        grid=(n // tm,),
        in_specs=[row(D_MODEL), full(w16), row(1), full(invf)],
        out_specs=[row(HA)] * 7 + [row(LANES)] + [row(HA)] * 4,
        out_shape=[wide] * 7 + [jax.ShapeDtypeStruct((n, LANES), F32)] + [wide16] * 4,
        compiler_params=_cparams("parallel"),
        name="ab_project",
    )(x2d, w16, pos, invf)


def _fox_proj_kernel(x_ref, w_ref, bf_ref, q_ref, k_ref, v_ref, logf_ref, cum_ref,
                     k16_ref, v16_ref, carry_ref, *, tiles_per_seq):
    i = pl.program_id(0)
    xb = x_ref[...].astype(BF16)
    q_ref[...] = _dot(xb, w_ref[:, 0:HC])
    k = _dot(xb, w_ref[:, HC:2 * HC])
    k_ref[...] = k
    k16_ref[...] = k.astype(BF16)
    v = _dot(xb, w_ref[:, 2 * HC:3 * HC])
    v_ref[...] = v
    v16_ref[...] = v.astype(BF16)
    f = _dot(xb, w_ref[:, 3 * HC:3 * HC + LANES])
    logf = _log_sigmoid(f + bf_ref[...])
    logf_ref[...] = logf

    @pl.when(i % tiles_per_seq == 0)
    def _():
        carry_ref[...] = jnp.zeros_like(carry_ref)

    tm = logf.shape[0]
    r = lax.broadcasted_iota(I32, (tm, tm), 0)
    cc = lax.broadcasted_iota(I32, (tm, tm), 1)
    tril = jnp.where(cc <= r, 1.0, 0.0).astype(BF16)
    hi, mid, lo = _split3(logf)
    cum = (_dot(tril, hi) + _dot(tril, mid) + _dot(tril, lo)) + carry_ref[...]
    cum_ref[...] = cum
    carry_ref[...] = cum[tm - 1:tm, :]


def _fox_project(x2d, w16, bf_pad, tm, rows_per_seq):
    n = x2d.shape[0]
    wide = jax.ShapeDtypeStruct((n, HC), F32)
    wide16 = jax.ShapeDtypeStruct((n, HC), BF16)
    small = jax.ShapeDtypeStruct((n, LANES), F32)
    row = lambda w: pl.BlockSpec((tm, w), lambda i: (i, 0))
    full = lambda a: pl.BlockSpec(a.shape, lambda i: (0, 0))
    return pl.pallas_call(
        functools.partial(_fox_proj_kernel, tiles_per_seq=rows_per_seq // tm),
        grid=(n // tm,),
        in_specs=[row(D_MODEL), full(w16), full(bf_pad)],
        out_specs=[row(HC)] * 3 + [row(LANES)] * 2 + [row(HC)] * 2,
        out_shape=[wide] * 3 + [small] * 2 + [wide16] * 2,
        scratch_shapes=[pltpu.VMEM((1, LANES), F32)],
        compiler_params=_cparams("arbitrary"),
        name="fox_project",
    )(x2d, w16, bf_pad)


def _out_ln_kernel(o_ref, w_ref, x_ref, g_ref, b_ref, y_ref):
    m = _dot(o_ref[...], w_ref[...])
    y_ref[...] = _layer_norm(ALPHA * x_ref[...] + m, g_ref[...], b_ref[...])


def _out_proj_ln(o16, w16, x2d, g, b, tm):
    n, k = o16.shape
    row = lambda w: pl.BlockSpec((tm, w), lambda i: (i, 0))
    full = lambda a: pl.BlockSpec(a.shape, lambda i: (0, 0))
    return pl.pallas_call(
        _out_ln_kernel,
        grid=(n // tm,),
        in_specs=[row(k), full(w16), row(D_MODEL), full(g), full(b)],
        out_specs=row(D_MODEL),
        out_shape=jax.ShapeDtypeStruct((n, D_MODEL), F32),
        compiler_params=_cparams("parallel"),
        name="out_proj_ln",
    )(o16, w16, x2d, g, b)


def _softmax_init(m_scr, l_scr, acc_scr):
    m_scr[...] = jnp.full(m_scr.shape, NEG, F32)
    l_scr[...] = jnp.zeros(l_scr.shape, F32)
    acc_scr[...] = jnp.zeros(acc_scr.shape, F32)


def _softmax_step(h, s, vt, m_scr, l_scr, acc_scr):
    m_old = m_scr[h]
    m_new = jnp.maximum(m_old, jnp.max(s, axis=0, keepdims=True))
    alpha = jnp.exp(m_old - m_new)
    p = jnp.exp(s - m_new)
    l_scr[h] = alpha * l_scr[h] + jnp.sum(p, axis=0, keepdims=True)
    acc_scr[h] = alpha * acc_scr[h] + _dot(vt, p.astype(BF16))
    m_scr[h] = m_new


def _pair_masks(shape):
    lane = lax.broadcasted_iota(I32, shape, 1)
    return lane < HEAD_DIM


def _store_pair_queries(q, qm_scr, base):
    lo_head = _pair_masks(q.shape)
    qs = (q * QK_SCALE).astype(BF16)
    zero = jnp.zeros_like(qs)
    qm_scr[base] = jnp.where(lo_head, qs, zero)
    qm_scr[base + 1] = jnp.where(lo_head, zero, qs)


def _pair_output(hp, l_scr, acc_scr):
    a0 = acc_scr[2 * hp] / l_scr[2 * hp]
    a1 = acc_scr[2 * hp + 1] / l_scr[2 * hp + 1]
    first = lax.broadcasted_iota(I32, a0.shape, 0) < HEAD_DIM
    return jnp.where(first, a0, a1).T.astype(BF16)


def _key_query_index(tk, tq, k0, q0):
    kidx = k0 + lax.broadcasted_iota(I32, (tk, tq), 0)
    qidx = q0 + lax.broadcasted_iota(I32, (tk, tq), 1)
    return kidx, qidx


def _sortable_key(x):
    b = pltpu.bitcast(x, I32)
    return jnp.where(b < 0, INT_MIN - b, b)


def _count_keys(key_scr, n_tiles, pred):
    _, tk, tq = key_scr.shape

    def body(j, acc):
        for cidx in range(tk // 8):
            chunk = key_scr[j, cidx * 8:(cidx + 1) * 8, :]
            acc = acc + jnp.where(pred(chunk, j * tk + cidx * 8), 1, 0)
        return acc

    acc = lax.fori_loop(0, n_tiles, body, jnp.zeros((8, tq), I32))
    return jnp.sum(acc, axis=0, keepdims=True)


def _topk_threshold(key_scr, n_tiles, topk, key_bits):
    _, _, tq = key_scr.shape
    rows8 = lambda x: jnp.broadcast_to(x, (8, tq))

    def bit_body(i, t):
        cand = rows8(t + jnp.left_shift(jnp.int32(1), 31 - i))
        cnt = _count_keys(key_scr, n_tiles, lambda k, k0: k >= cand)
        return jnp.where(cnt >= topk, cand[0:1, :], t)

    t = lax.fori_loop(0, 32, bit_body, jnp.full((1, tq), INT_MIN, I32))
    t8 = rows8(t)
    n_gt = _count_keys(key_scr, n_tiles, lambda k, k0: k > t8)
    n_eq = _count_keys(key_scr, n_tiles, lambda k, k0: k == t8)
    need = topk - n_gt
    excess = jnp.max(jnp.where(n_eq > need, 1, 0)) > 0

    def tie_search():
        sub = lax.broadcasted_iota(I32, (8, tq), 0)

        def body(i, x):
            cand = rows8(x + jnp.left_shift(jnp.int32(1), key_bits - 1 - i))
            cnt = _count_keys(key_scr, n_tiles, lambda k, k0: (k == t8) & (sub + k0 < cand))
            return jnp.where(cnt < need, cand[0:1, :], x)

        return lax.fori_loop(0, key_bits, body, jnp.zeros((1, tq), I32))

    c = lax.cond(excess, tie_search, lambda: jnp.full((1, tq), 2 ** 31 - 1, I32))
    c = jnp.where(t == INT_MIN, -1, c)
    return t, c


def _dsa_prompt_kernel(qa_ref, qi_ref, wt_ref, kic_ref, k_ref, vt_ref, o_ref,
                       key_scr, qic_scr, qm_scr, m_scr, l_scr, acc_scr,
                       *, tq, tk, topk, key_bits):
    qt = pl.program_id(1)
    q0 = qt * tq
    n_tiles = (q0 + tq + tk - 1) // tk

    qi = qi_ref[...]
    for h in range(IDX_HEADS):
        hi, lo = _split2(qi[:, h * IDX_DIM:(h + 1) * IDX_DIM])
        qic_scr[h] = jnp.concatenate([hi, hi, lo, lo], axis=1)
    wt = wt_ref[...]

    def score_tile(j, diagonal):
        kc = kic_ref[pl.ds(j * tk, tk), :]
        sc = jnp.zeros((tk, tq), F32)
        for h in range(IDX_HEADS):
            d = _dot_nt(kc, qic_scr[h])
            sc = sc + wt[h:h + 1, :] * jnp.maximum(d, 0.0)
        key = _sortable_key(sc * IDX_SCALE)
        if diagonal:
            kidx, qidx = _key_query_index(tk, tq, j * tk, q0)
            key = jnp.where(kidx <= qidx, key, INT_MIN)
        key_scr[j] = key

    lax.fori_loop(0, n_tiles - 1, lambda j, _: (score_tile(j, False), 0)[1], 0)
    score_tile(n_tiles - 1, True)

    t, c = _topk_threshold(key_scr, n_tiles, topk, key_bits)

    qa = qa_ref[...]
    for hp in range(N_HEADS_A // 2):
        _store_pair_queries(qa[:, hp * LANES:(hp + 1) * LANES], qm_scr, 2 * hp)
    _softmax_init(m_scr, l_scr, acc_scr)

    def attn_tile(j, _):
        kt = key_scr[j]
        kidx = j * tk + lax.broadcasted_iota(I32, (tk, tq), 0)
        msk = (kt > t) | ((kt == t) & (kidx <= c))
        for hp in range(N_HEADS_A // 2):
            k = k_ref[pl.ds(j * tk, tk), hp * LANES:(hp + 1) * LANES]
            vt = vt_ref[hp, j]
            for h in (2 * hp, 2 * hp + 1):
                s = jnp.where(msk, _dot_nt(k, qm_scr[h]), NEG)
                _softmax_step(h, s, vt, m_scr, l_scr, acc_scr)
        return 0

    lax.fori_loop(0, n_tiles, attn_tile, 0)
    for hp in range(N_HEADS_A // 2):
        o_ref[:, hp * LANES:(hp + 1) * LANES] = _pair_output(hp, l_scr, acc_scr)


def _dsa_prompt(qa, qi, wt, ki_cat, k16, vt16, batch, seq, tq, tk):
    nq = seq // tq
    nk = seq // tk
    npair = N_HEADS_A // 2
    topk = min(DSA_TOPK, seq // 4)
    key_bits = max(1, (seq - 1).bit_length())
    once = pl.Buffered(1)
    qrow = lambda w: pl.BlockSpec((tq, w), lambda b, i: (b * nq + i, 0))
    return pl.pallas_call(
        functools.partial(_dsa_prompt_kernel, tq=tq, tk=tk, topk=topk, key_bits=key_bits),
        grid=(batch, nq),
        in_specs=[
            qrow(HA), qrow(HA),
            pl.BlockSpec((None, IDX_HEADS, tq), lambda b, i: (b, 0, i)),
            pl.BlockSpec((seq, 4 * IDX_DIM), lambda b, i: (b, 0), pipeline_mode=once),
            pl.BlockSpec((seq, HA), lambda b, i: (b, 0), pipeline_mode=once),
            pl.BlockSpec((None, npair, nk, LANES, tk), lambda b, i: (b, 0, 0, 0, 0), pipeline_mode=once),
        ],
        out_specs=qrow(HA),
        out_shape=jax.ShapeDtypeStruct((batch * seq, HA), BF16),
        scratch_shapes=[
            pltpu.VMEM((nk, tk, tq), I32),
            pltpu.VMEM((IDX_HEADS, tq, 4 * IDX_DIM), BF16),
            pltpu.VMEM((N_HEADS_A, tq, LANES), BF16),
            pltpu.VMEM((N_HEADS_A, 1, tq), F32),
            pltpu.VMEM((N_HEADS_A, 1, tq), F32),
            pltpu.VMEM((N_HEADS_A, LANES, tq), F32),
        ],
        compiler_params=_cparams("parallel", "parallel"),
        name="dsa_prompt",
    )(qa, qi, wt, ki_cat, k16, vt16)


def _block_mean_kernel(k_ref, o_ref):
    o_ref[...] = jnp.mean(k_ref[...], axis=0, keepdims=True)[None]


def _block_means(k2d):
    n = k2d.shape[0]
    nb = n // MOBA_BLOCK
    return pl.pallas_call(
        _block_mean_kernel,
        grid=(nb,),
        in_specs=[pl.BlockSpec((MOBA_BLOCK, HB), lambda i: (i, 0))],
        out_specs=pl.BlockSpec((1, 1, HB), lambda i: (i, 0, 0)),
        out_shape=jax.ShapeDtypeStruct((nb, 1, HB), F32),
        compiler_params=_cparams("parallel"),
        name="moba_block_means",
    )(k2d)


def _moba_prompt_kernel(q_ref, km_ref, k_ref, vt_ref, o_ref,
                        sel_scr, qm_scr, m_scr, l_scr, acc_scr, *, t):
    qt = pl.program_id(2)
    q = q_ref[...]
    nb = km_ref.shape[0]
    lo_head = _pair_masks(q.shape)
    km_hi, km_lo = _split2(km_ref[...])
    blk = lax.broadcasted_iota(I32, (nb, t), 0)
    for hh in range(2):
        qh = jnp.where(lo_head if hh == 0 else ~lo_head, q, 0.0)
        q_hi, q_lo = _split2(qh)
        g = _dot3(km_hi, km_lo, q_hi, q_lo, _dot_nt)
        g = jnp.where(blk < qt, g, -jnp.inf)
        sel = jnp.zeros((nb, t), F32)
        for _ in range(MOBA_TOPK):
            mx = jnp.max(g, axis=0, keepdims=True)
            first = jnp.min(jnp.where(g == mx, blk, nb), axis=0, keepdims=True)
            pick = (blk == first) & (mx > -jnp.inf)
            sel = jnp.where(pick, 1.0, sel)
            g = jnp.where(blk == first, -jnp.inf, g)
        sel_scr[hh] = sel
    _store_pair_queries(q, qm_scr, 0)
    _softmax_init(m_scr, l_scr, acc_scr)

    def past_tile(j, _):
        k = k_ref[pl.ds(j * t, t), :]
        vt = vt_ref[j]
        for hh in range(2):
            chosen = sel_scr[hh, pl.ds(j, 1), :] > 0.0
            s = jnp.where(chosen, _dot_nt(k, qm_scr[hh]), NEG)
            _softmax_step(hh, s, vt, m_scr, l_scr, acc_scr)
        return 0

    lax.fori_loop(0, qt, past_tile, 0)
    k = k_ref[pl.ds(qt * t, t), :]
    vt = vt_ref[qt]
    kidx, qidx = _key_query_index(t, t, 0, 0)
    for hh in range(2):
        s = jnp.where(kidx <= qidx, _dot_nt(k, qm_scr[hh]), NEG)
        _softmax_step(hh, s, vt, m_scr, l_scr, acc_scr)
    o_ref[...] = _pair_output(0, l_scr, acc_scr)


def _moba_prompt(q, kmean, k16, vt16, batch, seq):
    t = MOBA_T
    nq = seq // t
    nb = seq // MOBA_BLOCK
    npair = N_HEADS_B // 2
    return pl.pallas_call(
        functools.partial(_moba_prompt_kernel, t=t),
        grid=(batch, npair, nq),
        in_specs=[
            pl.BlockSpec((t, LANES), lambda b, p, i: (b * nq + i, p)),
            pl.BlockSpec((None, nb, LANES), lambda b, p, i: (b, 0, p)),
            pl.BlockSpec((seq, LANES), lambda b, p, i: (b, p)),
            pl.BlockSpec((None, None, nq, LANES, t), lambda b, p, i: (b, p, 0, 0, 0)),
        ],
        out_specs=pl.BlockSpec((t, LANES), lambda b, p, i: (b * nq + i, p)),
        out_shape=jax.ShapeDtypeStruct((batch * seq, HB), BF16),
        scratch_shapes=[
            pltpu.VMEM((2, nb, t), F32),
            pltpu.VMEM((2, t, LANES), BF16),
            pltpu.VMEM((2, 1, t), F32),
            pltpu.VMEM((2, 1, t), F32),
            pltpu.VMEM((2, LANES, t), F32),
        ],
        compiler_params=_cparams("parallel", "parallel", "parallel"),
        name="moba_prompt",
    )(q, kmean, k16, vt16)


def _fox_prompt_kernel(q_ref, cq_ref, ck_ref, k_ref, vt_ref, o_ref,
                       qm_scr, m_scr, l_scr, acc_scr, *, tq, tk):
    qt = pl.program_id(2)
    q0 = qt * tq
    _store_pair_queries(q_ref[...], qm_scr, 0)
    _softmax_init(m_scr, l_scr, acc_scr)
    cq = cq_ref[...]

    def tile(j, diagonal):
        k = k_ref[pl.ds(j * tk, tk), :]
        vt = vt_ref[j]
        ck = ck_ref[pl.ds(j * tk, tk), :]
        if diagonal:
            kidx, qidx = _key_query_index(tk, tq, j * tk, q0)
        logits = [_dot_nt(k, qm_scr[hh]) for hh in range(2)]
        for hh in range(2):
            s = logits[hh] + (cq[hh:hh + 1, :] - ck[:, hh:hh + 1])
            if diagonal:
                s = jnp.where(kidx <= qidx, s, NEG)
            _softmax_step(hh, s, vt, m_scr, l_scr, acc_scr)

    n_full = q0 // tk
    lax.fori_loop(0, n_full, lambda j, _: (tile(j, False), 0)[1], 0)
    tile(n_full, True)
    o_ref[...] = _pair_output(0, l_scr, acc_scr)


def _fox_prompt(q, cum_q, cum_k, k16, vt16, batch, seq, tq, tk):
    nq = seq // tq
    nk = seq // tk
    npair = N_HEADS_C // 2
    return pl.pallas_call(
        functools.partial(_fox_prompt_kernel, tq=tq, tk=tk),
        grid=(batch, npair, nq),
        in_specs=[
            pl.BlockSpec((tq, LANES), lambda b, p, i: (b * nq + i, p)),
            pl.BlockSpec((None, None, 2, tq), lambda b, p, i: (b, p, 0, i)),
            pl.BlockSpec((None, None, seq, 2), lambda b, p, i: (b, p, 0, 0)),
            pl.BlockSpec((seq, LANES), lambda b, p, i: (b, p)),
            pl.BlockSpec((None, None, nk, LANES, tk), lambda b, p, i: (b, p, 0, 0, 0)),
        ],
        out_specs=pl.BlockSpec((tq, LANES), lambda b, p, i: (b * nq + i, p)),
        out_shape=jax.ShapeDtypeStruct((batch * seq, HC), BF16),
        scratch_shapes=[
            pltpu.VMEM((2, tq, LANES), BF16),
            pltpu.VMEM((2, 1, tq), F32),
            pltpu.VMEM((2, 1, tq), F32),
            pltpu.VMEM((2, LANES, tq), F32),
        ],
        compiler_params=_cparams("parallel", "parallel", "parallel"),
        name="fox_prompt",
    )(q, cum_q, cum_k, k16, vt16)


def _router_kernel(x_ref, rwh_ref, rwl_ref, rb_ref, g_ref):
    x_hi, x_lo = _split2(x_ref[...])
    logits = _dot3(rwh_ref[...], rwl_ref[...], x_hi, x_lo, _dot_nt)
    scores = 1.0 / (1.0 + jnp.exp(-logits))
    biased = scores + rb_ref[...]
    rows = [biased[e:e + 1, :] for e in range(N_EXPERTS)]
    ninf = jnp.full_like(rows[0], -jnp.inf)

    def top2(vals):
        mx = functools.reduce(jnp.maximum, vals)
        picks1, found = [], jnp.zeros_like(mx) > 0
        for vv in vals:
            p = (vv == mx) & ~found
            found = found | p
            picks1.append(p)
        rest = [jnp.where(p, ninf, vv) for p, vv in zip(picks1, vals)]
        mx2 = functools.reduce(jnp.maximum, rest)
        picks2, found = [], jnp.zeros_like(mx) > 0
        for vv in rest:
            p = (vv == mx2) & ~found
            found = found | p
            picks2.append(p)
        return mx, mx2, picks1, picks2

    grp_score = []
    for gi in range(N_GROUPS):
        m1, m2, _, _ = top2(rows[gi * EXPERTS_PER_GROUP:(gi + 1) * EXPERTS_PER_GROUP])
        grp_score.append(m1 + m2)
    best = grp_score[0]
    g_sel = jnp.zeros_like(best, dtype=I32)
    for gi in range(1, N_GROUPS):
        better = grp_score[gi] > best
        best = jnp.where(better, grp_score[gi], best)
        g_sel = jnp.where(better, gi, g_sel)
    masked = [jnp.where(g_sel == e // EXPERTS_PER_GROUP, rows[e], ninf) for e in range(N_EXPERTS)]
    _, _, p1, p2 = top2(masked)
    zero = jnp.zeros_like(best)
    w1 = functools.reduce(jnp.add, [jnp.where(p1[e], scores[e:e + 1, :], zero) for e in range(N_EXPERTS)])
    w2 = functools.reduce(jnp.add, [jnp.where(p2[e], scores[e:e + 1, :], zero) for e in range(N_EXPERTS)])
    tot = w1 + w2
    for e in range(N_EXPERTS):
        g_ref[e:e + 1, :] = jnp.where(p1[e], w1 / tot, zero) + jnp.where(p2[e], w2 / tot, zero)


def _router(x2d, rwt_hi, rwt_lo, rb, tm):
    n = x2d.shape[0]
    full = lambda a: pl.BlockSpec(a.shape, lambda i: (0, 0))
    return pl.pallas_call(
        _router_kernel,
        grid=(n // tm,),
        in_specs=[pl.BlockSpec((tm, D_MODEL), lambda i: (i, 0)), full(rwt_hi), full(rwt_lo), full(rb)],
        out_specs=pl.BlockSpec((N_EXPERTS, tm), lambda i: (0, i)),
        out_shape=jax.ShapeDtypeStruct((N_EXPERTS, n), F32),
        compiler_params=_cparams("parallel"),
        name="moe_router",
    )(x2d, rwt_hi, rwt_lo, rb)


def _moe_kernel(x_ref, gates_ref, wg_ref, wu_ref, wd_ref, g_ref, b_ref, y_ref, xb_scr, acc_scr):
    e = pl.program_id(1)

    @pl.when(e == 0)
    def _():
        xb_scr[...] = x_ref[...].astype(BF16)
        acc_scr[...] = jnp.zeros_like(acc_scr)

    xb = xb_scr[...]
    gate = _dot(xb, wg_ref[0])
    up = _dot(xb, wu_ref[0])
    h = (gate * (1.0 / (1.0 + jnp.exp(-gate))) * up).astype(BF16)
    down = _dot(h, wd_ref[0])
    gates = gates_ref[...]
    lane = lax.broadcasted_iota(I32, gates.shape, 1)
    w = jnp.sum(jnp.where(lane == e, gates, 0.0), axis=1, keepdims=True)
    acc_scr[...] += w * down

    @pl.when(e == pl.num_programs(1) - 1)
    def _():
        y_ref[...] = _layer_norm(ALPHA * x_ref[...] + acc_scr[...], g_ref[...], b_ref[...])


def _moe_ln(x2d, gates, wg16, wu16, wd16, g, b, tm):
    n = x2d.shape[0]
    full = lambda a: pl.BlockSpec(a.shape, lambda i, e: (0, 0))
    return pl.pallas_call(
        _moe_kernel,
        grid=(n // tm, N_EXPERTS),
        in_specs=[
            pl.BlockSpec((tm, D_MODEL), lambda i, e: (i, 0)),
            pl.BlockSpec((tm, N_EXPERTS), lambda i, e: (i, 0)),
            pl.BlockSpec((1, D_MODEL, D_EXPERT), lambda i, e: (e, 0, 0)),
            pl.BlockSpec((1, D_MODEL, D_EXPERT), lambda i, e: (e, 0, 0)),
            pl.BlockSpec((1, D_EXPERT, D_MODEL), lambda i, e: (e, 0, 0)),
            full(g), full(b),
        ],
        out_specs=pl.BlockSpec((tm, D_MODEL), lambda i, e: (i, 0)),
        out_shape=jax.ShapeDtypeStruct((n, D_MODEL), F32),
        scratch_shapes=[pltpu.VMEM((tm, D_MODEL), BF16), pltpu.VMEM((tm, D_MODEL), F32)],
        compiler_params=_cparams("parallel", "arbitrary"),
        name="moe_experts_ln",
    )(x2d, gates, wg16, wu16, wd16, g, b)


def _dsa_sample_score_kernel(pt_ref, q_ref, w_ref, knew_ref, *rest, n_pages):
    page_refs = rest[:n_pages]
    s_ref = rest[n_pages]
    q = q_ref[...]
    q_hi, q_lo = _split2(q)
    w = w_ref[...]
    for p in range(n_pages):
        k_hi, k_lo = _split2(page_refs[p][...])
        d = _dot3(q_hi, q_lo, k_hi, k_lo, _dot_nt)
        s_ref[p:p + 1, :] = jnp.sum(w * jnp.maximum(d, 0.0), axis=0, keepdims=True) * IDX_SCALE
    k_new = knew_ref[...][:, 0:IDX_DIM]
    d_new = jnp.sum(q * k_new, axis=1, keepdims=True)
    s_new = jnp.sum(w * jnp.maximum(d_new, 0.0), axis=0, keepdims=True) * IDX_SCALE
    lane = lax.broadcasted_iota(I32, (1, PAGE_SIZE), 1)
    s_ref[n_pages:n_pages + 1, :] = jnp.where(lane == 0, s_new, -jnp.inf)


def _dsa_sample_scores(page_table, q_i, w_i, tail, cache_idx):
    nseq, n_pages = page_table.shape
    page = lambda p: pl.BlockSpec((None, PAGE_SIZE, IDX_DIM), lambda b, pt: (pt[b, p], 0, 0))
    grid_spec = pltpu.PrefetchScalarGridSpec(
        num_scalar_prefetch=1,
        grid=(nseq,),
        in_specs=[
            pl.BlockSpec((None, IDX_HEADS, IDX_DIM), lambda b, pt: (b, 0, 0)),
            pl.BlockSpec((None, IDX_HEADS, 1), lambda b, pt: (b, 0, 0)),
            pl.BlockSpec((None, 1, LANES), lambda b, pt: (b, 0, 0)),
        ] + [page(p) for p in range(n_pages)],
        out_specs=pl.BlockSpec((None, n_pages + 1, PAGE_SIZE), lambda b, pt: (b, 0, 0)),
    )
    return pl.pallas_call(
        functools.partial(_dsa_sample_score_kernel, n_pages=n_pages),
        grid_spec=grid_spec,
        out_shape=jax.ShapeDtypeStruct((nseq, n_pages + 1, PAGE_SIZE), F32),
        compiler_params=_cparams("parallel"),
        name="dsa_sample_scores",
    )(page_table, q_i, w_i, tail, *([cache_idx] * n_pages))


def _select_kernel(s_ref, valid_ref, bias_ref, key_scr, *, topk, key_bits):
    n_tiles, tk, nseq = s_ref.shape
    for j in range(n_tiles):
        key_scr[j] = jnp.where(valid_ref[j] > 0.0, _sortable_key(s_ref[j]), INT_MIN)
    t, c = _topk_threshold(key_scr, n_tiles, topk, key_bits)
    for j in range(n_tiles):
        kt = key_scr[j]
        kidx = j * tk + lax.broadcasted_iota(I32, kt.shape, 0)
        bias_ref[j] = jnp.where((kt > t) | ((kt == t) & (kidx <= c)), 0.0, NEG)


def _select_topk_bias(scores_t, valid_t, topk):
    n_chunks = scores_t.shape[0]
    key_bits = max(1, (n_chunks * PAGE_SIZE - 1).bit_length())
    return pl.pallas_call(
        functools.partial(_select_kernel, topk=topk, key_bits=key_bits),
        out_shape=jax.ShapeDtypeStruct(scores_t.shape, F32),
        scratch_shapes=[pltpu.VMEM(scores_t.shape, I32)],
        compiler_params=pltpu.CompilerParams(vmem_limit_bytes=VMEM_LIMIT),
        name="dsa_sample_select",
    )(scores_t, valid_t)


def _own_head(n_heads, n_rows):
    r = lax.broadcasted_iota(I32, (n_heads, n_rows), 0)
    col = lax.broadcasted_iota(I32, (n_heads, n_rows), 1)
    return col % n_heads == r


def _page_rows(ref):
    keys, n_heads, dim = ref.shape
    return ref[...].reshape(keys * n_heads, dim).astype(BF16)


def _decode_logits(q_ref, k_ref):
    n_heads = q_ref.shape[0]
    qs = (q_ref[...] * QK_SCALE).astype(BF16)
    s = _dot_nt(qs, _page_rows(k_ref))
    return s, _own_head(n_heads, k_ref.shape[0] * n_heads)


def _decode_init(m_scr, l_scr, acc_scr):
    m_scr[...] = jnp.full(m_scr.shape, NEG, F32)
    l_scr[...] = jnp.zeros(l_scr.shape, F32)
    acc_scr[...] = jnp.zeros(acc_scr.shape, F32)


def _decode_step(s, v, m_scr, l_scr, acc_scr):
    m_old = m_scr[...]
    m_new = jnp.maximum(m_old, jnp.max(s, axis=1, keepdims=True))
    alpha = jnp.exp(m_old - m_new)
    p = jnp.exp(s - m_new)
    l_scr[...] = alpha * l_scr[...] + jnp.sum(p, axis=1, keepdims=True)
    acc_scr[...] = alpha * acc_scr[...] + _dot(p.astype(BF16), v)
    m_scr[...] = m_new


def _decode_finish(q, bias_new, k_new, v_new, m_scr, l_scr, acc_scr):
    s_new = jnp.sum((q * QK_SCALE) * k_new, axis=1, keepdims=True) + bias_new
    m_old = m_scr[...]
    m_new = jnp.maximum(m_old, s_new)
    alpha = jnp.exp(m_old - m_new)
    p_new = jnp.exp(s_new - m_new)
    l = alpha * l_scr[...] + p_new
    acc = alpha * acc_scr[...] + p_new * v_new
    return acc / l


def _dsa_sample_attn_kernel(pt_ref, q_ref, bias_ref, knew_ref, vnew_ref, k_ref, v_ref, o_ref,
                            m_scr, l_scr, acc_scr):
    p = pl.program_id(1)

    @pl.when(p == 0)
    def _():
        _decode_init(m_scr, l_scr, acc_scr)

    s, own = _decode_logits(q_ref, k_ref)
    s = jnp.where(own, s + bias_ref[pl.ds(p, 1), :], NEG)
    _decode_step(s, _page_rows(v_ref), m_scr, l_scr, acc_scr)

    @pl.when(p == pl.num_programs(1) - 1)
    def _():
        b_new = bias_ref[pl.ds(p + 1, 1), :][:, 0:1]
        o_ref[...] = _decode_finish(q_ref[...], b_new, knew_ref[...], vnew_ref[...],
                                    m_scr, l_scr, acc_scr).astype(BF16)


def _dsa_sample_attn(page_table, q, bias, k_new, v_new, cache_k, cache_v):
    nseq, n_pages = page_table.shape
    rows = PAGE_SIZE * N_HEADS_A
    hd = pl.BlockSpec((None, N_HEADS_A, HEAD_DIM), lambda b, p, pt: (b, 0, 0))
    page = pl.BlockSpec((None, PAGE_SIZE, N_HEADS_A, HEAD_DIM), lambda b, p, pt: (pt[b, p], 0, 0, 0))
    grid_spec = pltpu.PrefetchScalarGridSpec(
        num_scalar_prefetch=1,
        grid=(nseq, n_pages),
        in_specs=[hd, pl.BlockSpec((None, n_pages + 1, rows), lambda b, p, pt: (b, 0, 0)),
                  hd, hd, page, page],
        out_specs=hd,
        scratch_shapes=[pltpu.VMEM((N_HEADS_A, 1), F32), pltpu.VMEM((N_HEADS_A, 1), F32),
                        pltpu.VMEM((N_HEADS_A, HEAD_DIM), F32)],
    )
    return pl.pallas_call(
        _dsa_sample_attn_kernel,
        grid_spec=grid_spec,
        out_shape=jax.ShapeDtypeStruct((nseq, N_HEADS_A, HEAD_DIM), BF16),
        compiler_params=_cparams("parallel", "arbitrary"),
        name="dsa_sample_attn",
    )(page_table, q, bias, k_new, v_new, cache_k, cache_v)


def _moba_sample_kernel(pt_ref, q_ref, knew_ref, vnew_ref, k_ref, v_ref, o_ref,
                        ksum_scr, gate_scr, mb_scr, lb_scr, accb_scr,
                        m_scr, l_scr, acc_scr, *, pages_per_block):
    p = pl.program_id(1)
    n_pages = pl.num_programs(1)
    n_blocks = gate_scr.shape[0]

    @pl.when(p % pages_per_block == 0)
    def _():
        ksum_scr[...] = jnp.zeros(ksum_scr.shape, F32)
        _decode_init(m_scr, l_scr, acc_scr)

    ksum_scr[...] += jnp.sum(k_ref[...], axis=0)
    s, own = _decode_logits(q_ref, k_ref)
    _decode_step(jnp.where(own, s, NEG), _page_rows(v_ref), m_scr, l_scr, acc_scr)

    @pl.when(p % pages_per_block == pages_per_block - 1)
    def _():
        n = p // pages_per_block
        k_mean = ksum_scr[...] / float(pages_per_block * PAGE_SIZE)
        gate_scr[n] = jnp.sum(q_ref[...] * k_mean, axis=1, keepdims=True)
        mb_scr[n] = m_scr[...]
        lb_scr[n] = l_scr[...]
        accb_scr[n] = acc_scr[...]

    @pl.when(p == n_pages - 1)
    def _():
        gates = [gate_scr[n] for n in range(n_blocks)]
        chosen = [jnp.zeros_like(gates[0]) > 0 for _ in range(n_blocks)]
        for _ in range(min(MOBA_TOPK, n_blocks + 1)):
            mx = functools.reduce(jnp.maximum, gates)
            found = jnp.zeros_like(mx) > 0
            for n in range(n_blocks):
                pick = (gates[n] == mx) & ~found & (mx > -jnp.inf)
                found = found | pick
                chosen[n] = chosen[n] | pick
                gates[n] = jnp.where(pick, -jnp.inf, gates[n])
        s_new = jnp.sum((q_ref[...] * QK_SCALE) * knew_ref[...], axis=1, keepdims=True)
        m_tot = s_new
        for n in range(n_blocks):
            m_tot = jnp.maximum(m_tot, jnp.where(chosen[n], mb_scr[n], NEG))
        p_new = jnp.exp(s_new - m_tot)
        l = p_new
        acc = p_new * vnew_ref[...]
        for n in range(n_blocks):
            wgt = jnp.where(chosen[n], jnp.exp(mb_scr[n] - m_tot), 0.0)
            l = l + wgt * lb_scr[n]
            acc = acc + wgt * accb_scr[n]
        o_ref[...] = (acc / l).astype(BF16)


def _moba_sample(page_table, q, k_new, v_new, cache_k, cache_v):
    nseq, n_pages = page_table.shape
    ppb = MOBA_BLOCK // PAGE_SIZE
    n_blocks = n_pages // ppb
    rows = PAGE_SIZE * N_HEADS_B
    hd = pl.BlockSpec((None, N_HEADS_B, HEAD_DIM), lambda b, p, pt: (b, 0, 0))
    page = pl.BlockSpec((None, PAGE_SIZE, N_HEADS_B, HEAD_DIM), lambda b, p, pt: (pt[b, p], 0, 0, 0))
    stat = lambda *lead: pltpu.VMEM((*lead, N_HEADS_B, 1), F32)
    vec = lambda *lead: pltpu.VMEM((*lead, N_HEADS_B, HEAD_DIM), F32)
    grid_spec = pltpu.PrefetchScalarGridSpec(
        num_scalar_prefetch=1,
        grid=(nseq, n_pages),
        in_specs=[hd, hd, hd, page, page],
        out_specs=hd,
        scratch_shapes=[vec(), stat(n_blocks), stat(n_blocks), stat(n_blocks), vec(n_blocks),
                        stat(), stat(), vec()],
    )
    return pl.pallas_call(
        functools.partial(_moba_sample_kernel, pages_per_block=ppb),
        grid_spec=grid_spec,
        out_shape=jax.ShapeDtypeStruct((nseq, N_HEADS_B, HEAD_DIM), BF16),
        compiler_params=_cparams("parallel", "arbitrary"),
        name="moba_sample",
    )(page_table, q, k_new, v_new, cache_k, cache_v)


def _fox_sample_bias_kernel(pt_ref, lnew_ref, *rest, n_pages):
    page_refs = rest[:n_pages]
    bias_ref, bnew_ref = rest[n_pages], rest[n_pages + 1]
    r = lax.broadcasted_iota(I32, (PAGE_SIZE, PAGE_SIZE), 0)
    cc = lax.broadcasted_iota(I32, (PAGE_SIZE, PAGE_SIZE), 1)
    tril = jnp.where(cc <= r, 1.0, 0.0).astype(BF16)
    carry = jnp.zeros((1, N_HEADS_C), F32)
    cums = []
    for p in range(n_pages):
        hi, mid, lo = _split3(page_refs[p][...])
        cum = (_dot(tril, hi) + _dot(tril, mid) + _dot(tril, lo)) + carry
        cums.append(cum)
        carry = cum[PAGE_SIZE - 1:PAGE_SIZE, :]
    cum_q = carry + lnew_ref[...][:, 0:N_HEADS_C]
    for p in range(n_pages):
        bias_ref[p * PAGE_SIZE:(p + 1) * PAGE_SIZE, :] = cum_q - cums[p]
    bnew_ref[...] = cum_q - cum_q


def _fox_sample_bias(page_table, logf_new, cache_logf):
    nseq, n_pages = page_table.shape
    page = lambda p: pl.BlockSpec((None, PAGE_SIZE, N_HEADS_C), lambda b, pt: (pt[b, p], 0, 0))
    grid_spec = pltpu.PrefetchScalarGridSpec(
        num_scalar_prefetch=1,
        grid=(nseq,),
        in_specs=[pl.BlockSpec((None, 1, LANES), lambda b, pt: (b, 0, 0))] + [page(p) for p in range(n_pages)],
        out_specs=[pl.BlockSpec((None, n_pages * PAGE_SIZE, N_HEADS_C), lambda b, pt: (b, 0, 0)),
                   pl.BlockSpec((None, 1, N_HEADS_C), lambda b, pt: (b, 0, 0))],
    )
    return pl.pallas_call(
        functools.partial(_fox_sample_bias_kernel, n_pages=n_pages),
        grid_spec=grid_spec,
        out_shape=[jax.ShapeDtypeStruct((nseq, n_pages * PAGE_SIZE, N_HEADS_C), F32),
                   jax.ShapeDtypeStruct((nseq, 1, N_HEADS_C), F32)],
        compiler_params=_cparams("parallel"),
        name="fox_sample_bias",
    )(page_table, logf_new, *([cache_logf] * n_pages))


def _fox_sample_attn_kernel(pt_ref, q_ref, bias_ref, bnew_ref, knew_ref, vnew_ref, k_ref, v_ref, o_ref,
                            m_scr, l_scr, acc_scr):
    p = pl.program_id(1)

    @pl.when(p == 0)
    def _():
        _decode_init(m_scr, l_scr, acc_scr)

    s, own = _decode_logits(q_ref, k_ref)
    s = jnp.where(own, s + bias_ref[pl.ds(p, 1), :], NEG)
    _decode_step(s, _page_rows(v_ref), m_scr, l_scr, acc_scr)

    @pl.when(p == pl.num_programs(1) - 1)
    def _():
        o_ref[...] = _decode_finish(q_ref[...], bnew_ref[...], knew_ref[...], vnew_ref[...],
                                    m_scr, l_scr, acc_scr).astype(BF16)


def _fox_sample_attn(page_table, q, bias, bias_new, k_new, v_new, cache_k, cache_v):
    nseq, n_pages = page_table.shape
    rows = PAGE_SIZE * N_HEADS_C
    hd = pl.BlockSpec((None, N_HEADS_C, HEAD_DIM), lambda b, p, pt: (b, 0, 0))
    page = pl.BlockSpec((None, PAGE_SIZE, N_HEADS_C, HEAD_DIM), lambda b, p, pt: (pt[b, p], 0, 0, 0))
    grid_spec = pltpu.PrefetchScalarGridSpec(
        num_scalar_prefetch=1,
        grid=(nseq, n_pages),
        in_specs=[hd, pl.BlockSpec((None, n_pages, rows), lambda b, p, pt: (b, 0, 0)),
                  pl.BlockSpec((None, N_HEADS_C, 1), lambda b, p, pt: (b, 0, 0)),
                  hd, hd, page, page],
        out_specs=hd,
        scratch_shapes=[pltpu.VMEM((N_HEADS_C, 1), F32), pltpu.VMEM((N_HEADS_C, 1), F32),
                        pltpu.VMEM((N_HEADS_C, HEAD_DIM), F32)],
    )
    return pl.pallas_call(
        _fox_sample_attn_kernel,
        grid_spec=grid_spec,
        out_shape=jax.ShapeDtypeStruct((nseq, N_HEADS_C, HEAD_DIM), BF16),
        compiler_params=_cparams("parallel", "arbitrary"),
        name="fox_sample_attn",
    )(page_table, q, bias, bias_new, k_new, v_new, cache_k, cache_v)


def _pad_cols(w, width):
    return jnp.pad(w, ((0, 0), (0, width - w.shape[1])))


def _rope_inv_freq():
    half = HEAD_DIM // 8
    inv = ROPE_THETA ** (-jnp.arange(half, dtype=F32) / half)
    per_head = jnp.concatenate([inv, inv, jnp.zeros((HEAD_DIM - 2 * half,), F32)])
    return jnp.tile(per_head, LANES // HEAD_DIM)[None, :]


def _transposed_values(v16, batch, seq, tk):
    npair = v16.shape[1] // LANES
    return v16.reshape(batch, seq // tk, tk, npair, LANES).transpose(0, 3, 1, 4, 2)


def _ab_layer_prompt(x2d, w16, w_out16, invf, g, b, batch, seq):
    pos = jnp.tile(jnp.arange(seq, dtype=F32), batch)[:, None]
    qa, ka, va, qb, kb, vb, qi, tail, ka16, va16, kb16, vb16 = _ab_project(x2d, w16, pos, invf, PROJ_TM)
    k_idx = tail[:, :IDX_DIM]
    ki_hi, ki_lo = _split2(k_idx)
    ki_cat = jnp.concatenate([ki_hi, ki_lo, ki_hi, ki_lo], axis=1)
    wt = tail[:, IDX_DIM:IDX_DIM + IDX_HEADS].reshape(batch, seq, IDX_HEADS).transpose(0, 2, 1)
    o_a = _dsa_prompt(qa, qi, wt, ki_cat, ka16, _transposed_values(va16, batch, seq, DSA_TK),
                      batch, seq, DSA_TQ, DSA_TK)
    kmean = _block_means(kb).reshape(batch, seq // MOBA_BLOCK, HB)
    o_b = _moba_prompt(qb, kmean, kb16, _transposed_values(vb16, batch, seq, MOBA_T), batch, seq)
    o = jnp.concatenate([o_a, o_b], axis=1)
    y = _out_proj_ln(o, w_out16, x2d, g, b, PROJ_TM)
    return y, (ka, va, k_idx, kb, vb)


def _ab_layer_sample(x2d, w16, w_out16, invf, g, b, past_len, page_table,
                     cache_a_k, cache_a_v, cache_a_idx, cache_b_k, cache_b_v):
    nseq = x2d.shape[0]
    n_pages = page_table.shape[1]
    pos = jnp.full((nseq, 1), past_len, F32)
    qa, ka, va, qb, kb, vb, qi, tail, _, _, _, _ = _ab_project(x2d, w16, pos, invf, nseq)
    k_idx = tail[:, :IDX_DIM]
    w_i = tail[:, IDX_DIM:IDX_DIM + IDX_HEADS]
    heads = lambda a: a.reshape(nseq, -1, HEAD_DIM)
    scores = _dsa_sample_scores(page_table, heads(qi), w_i.reshape(nseq, IDX_HEADS, 1),
                                tail.reshape(nseq, 1, LANES), cache_a_idx)
    n_keys = n_pages * PAGE_SIZE + 1
    valid = (jnp.arange((n_pages + 1) * PAGE_SIZE) < n_keys).astype(F32)
    valid_t = jnp.broadcast_to(valid.reshape(n_pages + 1, PAGE_SIZE, 1), (n_pages + 1, PAGE_SIZE, nseq))
    bias_t = _select_topk_bias(scores.transpose(1, 2, 0), valid_t, min(DSA_TOPK, n_keys // 4))
    bias = jnp.repeat(bias_t.transpose(2, 0, 1), N_HEADS_A, axis=2)
    o_a = _dsa_sample_attn(page_table, heads(qa), bias, heads(ka), heads(va), cache_a_k, cache_a_v)
    o_b = _moba_sample(page_table, heads(qb), heads(kb), heads(vb), cache_b_k, cache_b_v)
    o = jnp.concatenate([o_a.reshape(nseq, HA), o_b.reshape(nseq, HB)], axis=1)
    y = _out_proj_ln(o, w_out16, x2d, g, b, nseq)
    return y, (ka, va, k_idx, kb, vb)


def _fox_layer_prompt(x2d, w16, bf_pad, w_out16, g, b, batch, seq):
    q, k, v, logf, cum, k16, v16 = _fox_project(x2d, w16, bf_pad, PROJ_TM, seq)
    npair = N_HEADS_C // 2
    cum_k = cum[:, :N_HEADS_C].reshape(batch, seq, npair, 2).transpose(0, 2, 1, 3)
    cum_q = cum_k.transpose(0, 1, 3, 2)
    o = _fox_prompt(q, cum_q, cum_k, k16, _transposed_values(v16, batch, seq, FOX_TK),
                    batch, seq, FOX_TQ, FOX_TK)
    y = _out_proj_ln(o, w_out16, x2d, g, b, PROJ_TM)
    return y, (k, v, logf[:, :N_HEADS_C])


def _fox_layer_sample(x2d, w16, bf_pad, w_out16, g, b, page_table, cache_c_k, cache_c_v, cache_c_logf):
    nseq, n_pages = page_table.shape
    q, k, v, logf, _, _, _ = _fox_project(x2d, w16, bf_pad, nseq, nseq)
    heads = lambda a: a.reshape(nseq, N_HEADS_C, HEAD_DIM)
    bias, bias_new = _fox_sample_bias(page_table, logf.reshape(nseq, 1, LANES), cache_c_logf)
    o = _fox_sample_attn(page_table, heads(q), bias.reshape(nseq, n_pages, PAGE_SIZE * N_HEADS_C),
                         bias_new.transpose(0, 2, 1), heads(k), heads(v), cache_c_k, cache_c_v)
    y = _out_proj_ln(o.reshape(nseq, HC), w_out16, x2d, g, b, nseq)
    return y, (k, v, logf[:, :N_HEADS_C])


def _moe_layer(x2d, rwt_hi, rwt_lo, rb, wg16, wu16, wd16, g, b, tm):
    gates_t = _router(x2d, rwt_hi, rwt_lo, rb, tm)
    return _moe_ln(x2d, gates_t.T, wg16, wu16, wd16, g, b, tm)


def kernel(x_prompt, x_sample, cache_a_k, cache_a_v, cache_a_idx, cache_b_k, cache_b_v, cache_c_k, cache_c_v, cache_c_logf, page_table, w_in_ab, w_out_ab, w_in_fox, b_forget, w_out_fox, ln_mix_g, ln_mix_b, ln_ffn_g, ln_ffn_b, router_w, router_bias, exp_w_gate, exp_w_up, exp_w_down):
    batch, seq, _ = x_prompt.shape
    nseq = x_sample.shape[0]
    past_len = page_table.shape[1] * PAGE_SIZE

    w_ab16 = _pad_cols(w_in_ab, 7 * HA + LANES).astype(BF16)
    w_out_ab16 = w_out_ab.astype(BF16)
    w_fox16 = _pad_cols(w_in_fox, 3 * HC + LANES).astype(BF16)
    w_out_fox16 = w_out_fox.astype(BF16)
    bf_pad = jnp.pad(b_forget, (0, LANES - N_HEADS_C))[None, :]
    invf = _rope_inv_freq()
    rwt_hi, rwt_lo = _split2(router_w.T)
    rb = router_bias[:, None]
    wg16, wu16, wd16 = exp_w_gate.astype(BF16), exp_w_up.astype(BF16), exp_w_down.astype(BF16)
    row = lambda a, i: a[i][None, :]

    xp = x_prompt.reshape(batch * seq, D_MODEL)
    xs = x_sample.reshape(nseq, D_MODEL)

    xp, (pa_k, pa_v, pa_idx, pb_k, pb_v) = _ab_layer_prompt(
        xp, w_ab16, w_out_ab16, invf, row(ln_mix_g, 0), row(ln_mix_b, 0), batch, seq)
    xs, (sa_k, sa_v, sa_idx, sb_k, sb_v) = _ab_layer_sample(
        xs, w_ab16, w_out_ab16, invf, row(ln_mix_g, 0), row(ln_mix_b, 0), past_len, page_table,
        cache_a_k, cache_a_v, cache_a_idx, cache_b_k, cache_b_v)
    xp = _moe_layer(xp, rwt_hi, rwt_lo, rb, wg16[0], wu16[0], wd16[0], row(ln_ffn_g, 0), row(ln_ffn_b, 0), MOE_TM)
    xs = _moe_layer(xs, rwt_hi, rwt_lo, rb, wg16[0], wu16[0], wd16[0], row(ln_ffn_g, 0), row(ln_ffn_b, 0), nseq)

    xp, (pc_k, pc_v, pc_logf) = _fox_layer_prompt(
        xp, w_fox16, bf_pad, w_out_fox16, row(ln_mix_g, 1), row(ln_mix_b, 1), batch, seq)
    xs, (sc_k, sc_v, sc_logf) = _fox_layer_sample(
        xs, w_fox16, bf_pad, w_out_fox16, row(ln_mix_g, 1), row(ln_mix_b, 1), page_table,
        cache_c_k, cache_c_v, cache_c_logf)
    xp = _moe_layer(xp, rwt_hi, rwt_lo, rb, wg16[1], wu16[1], wd16[1], row(ln_ffn_g, 1), row(ln_ffn_b, 1), MOE_TM)
    xs = _moe_layer(xs, rwt_hi, rwt_lo, rb, wg16[1], wu16[1], wd16[1], row(ln_ffn_g, 1), row(ln_ffn_b, 1), nseq)

    hd = lambda a, nh, lead: a.reshape(*lead, nh, HEAD_DIM)
    lp, ls = (batch, seq), (nseq, 1)
    return (xp.reshape(batch, seq, D_MODEL), xs.reshape(nseq, 1, D_MODEL),
            hd(pa_k, N_HEADS_A, lp), hd(pa_v, N_HEADS_A, lp), pa_idx.reshape(batch, seq, IDX_DIM),
            hd(pb_k, N_HEADS_B, lp), hd(pb_v, N_HEADS_B, lp),
            hd(pc_k, N_HEADS_C, lp), hd(pc_v, N_HEADS_C, lp), pc_logf.reshape(batch, seq, N_HEADS_C),
            hd(sa_k, N_HEADS_A, ls), hd(sa_v, N_HEADS_A, ls), sa_idx.reshape(nseq, 1, IDX_DIM),
            hd(sb_k, N_HEADS_B, ls), hd(sb_v, N_HEADS_B, ls),
            hd(sc_k, N_HEADS_C, ls), hd(sc_v, N_HEADS_C, ls), sc_logf.reshape(nseq, 1, N_HEADS_C))
```

```python
import functools

import jax
import jax.numpy as jnp
from jax import lax
from jax.experimental import pallas as pl
from jax.experimental.pallas import tpu as pltpu

F32 = jnp.float32
BF16 = jnp.bfloat16
I32 = jnp.int32

D_MODEL = 1024
DEPTH = 2
PAGE_SIZE = 128
HEAD_DIM = 64
N_HEADS_A = 8
N_HEADS_B = 8
N_HEADS_C = 16
IDX_HEADS = 8
IDX_DIM = 64
DSA_TOPK = 256
MOBA_BLOCK = 256
MOBA_TOPK = 3
ROPE_THETA = 500000.0
N_EXPERTS = 16
N_GROUPS = 4
EXPERTS_PER_GROUP = N_EXPERTS // N_GROUPS
D_EXPERT = 512
ALPHA = (2 * DEPTH) ** 0.25
LN_EPS = 1e-5
HA = N_HEADS_A * HEAD_DIM
HB = N_HEADS_B * HEAD_DIM
HC = N_HEADS_C * HEAD_DIM
QK_SCALE = HEAD_DIM ** -0.5
IDX_SCALE = IDX_DIM ** -0.5

LANES = 128
NEG = -1e30
INT_MIN = -2 ** 31
VMEM_LIMIT = 56 * 2 ** 20

PROJ_TM = 256
DSA_TQ = 256
DSA_TK = 512
MOBA_T = MOBA_BLOCK
FOX_TQ = 256
FOX_TK = 1024
MOE_TM = 512


def _cparams(*sem):
    return pltpu.CompilerParams(dimension_semantics=sem, vmem_limit_bytes=VMEM_LIMIT)


def _dot(a, b):
    return jnp.dot(a, b, preferred_element_type=F32)


def _dot_nt(a, b):
    return lax.dot_general(a, b, (((1,), (1,)), ((), ())), preferred_element_type=F32)


def _split2(x):
    hi = x.astype(BF16)
    lo = (x - hi.astype(F32)).astype(BF16)
    return hi, lo


def _split3(x):
    hi = x.astype(BF16)
    r = x - hi.astype(F32)
    mid = r.astype(BF16)
    lo = (r - mid.astype(F32)).astype(BF16)
    return hi, mid, lo


def _dot3(a_hi, a_lo, b_hi, b_lo, dot):
    return dot(a_hi, b_hi) + (dot(a_hi, b_lo) + dot(a_lo, b_hi))


def _lhs(x, w_ref):
    return _split2(x) if w_ref.dtype == F32 else (x.astype(BF16),)


def _mm(xs, w):
    if len(xs) == 1:
        return _dot(xs[0], w)
    w_hi, w_lo = _split2(w)
    return _dot3(xs[0], xs[1], w_hi, w_lo, _dot)


def _layer_norm(z, g, b):
    mu = jnp.mean(z, axis=-1, keepdims=True)
    d = z - mu
    var = jnp.mean(d * d, axis=-1, keepdims=True)
    return d * lax.rsqrt(var + LN_EPS) * g + b


def _log_sigmoid(z):
    return -(jnp.maximum(-z, 0.0) + jnp.log1p(jnp.exp(-jnp.abs(z))))


def _rotary_tables(pos_ref, invf_ref):
    ang = pos_ref[...] * invf_ref[...]
    c = jnp.cos(ang)
    s = jnp.sin(ang)
    f = lax.broadcasted_iota(I32, ang.shape, 1) % HEAD_DIM
    s_up = jnp.where(f < 8, -s, 0.0)
    s_dn = jnp.where(f >= 8, s, 0.0)
    return c, s_up, s_dn


def _rotate(h, c, s_up, s_dn):
    outs = []
    for j in range(h.shape[1] // LANES):
        hc = h[:, j * LANES:(j + 1) * LANES]
        outs.append(hc * c + pltpu.roll(hc, LANES - 8, 1) * s_up + pltpu.roll(hc, 8, 1) * s_dn)
    return outs[0] if len(outs) == 1 else jnp.concatenate(outs, axis=1)


def _ab_proj_kernel(x_ref, w_ref, pos_ref, invf_ref,
                    qa_ref, ka_ref, va_ref, qb_ref, kb_ref, vb_ref, qi_ref, tail_ref,
                    ka16_ref, va16_ref, kb16_ref, vb16_ref):
    xs = _lhs(x_ref[...], w_ref)
    c, s_up, s_dn = _rotary_tables(pos_ref, invf_ref)

    def seg(j, width=HA):
        return _mm(xs, w_ref[:, j * HA:j * HA + width])

    qa_ref[...] = _rotate(seg(0), c, s_up, s_dn)
    ka = _rotate(seg(1), c, s_up, s_dn)
    ka_ref[...] = ka
    ka16_ref[...] = ka.astype(BF16)
    va = seg(2)
    va_ref[...] = va
    va16_ref[...] = va.astype(BF16)
    qb_ref[...] = _rotate(seg(3), c, s_up, s_dn)
    kb = _rotate(seg(4), c, s_up, s_dn)
    kb_ref[...] = kb
    kb16_ref[...] = kb.astype(BF16)
    vb = seg(5)
    vb_ref[...] = vb
    vb16_ref[...] = vb.astype(BF16)
    qi_ref[...] = _rotate(seg(6), c, s_up, s_dn)
    t = seg(7, LANES)
    lane = lax.broadcasted_iota(I32, t.shape, 1)
    is_key = lane < IDX_DIM
    ct = jnp.where(is_key, c, IDX_HEADS ** -0.5)
    tail_ref[...] = (t * ct + pltpu.roll(t, LANES - 8, 1) * jnp.where(is_key, s_up, 0.0)
                     + pltpu.roll(t, 8, 1) * jnp.where(is_key, s_dn, 0.0))


def _ab_project(x2d, w16, pos, invf, tm):
    n = x2d.shape[0]
    wide = jax.ShapeDtypeStruct((n, HA), F32)
    wide16 = jax.ShapeDtypeStruct((n, HA), BF16)
    row = lambda w: pl.BlockSpec((tm, w), lambda i: (i, 0))
    full = lambda a: pl.BlockSpec(a.shape, lambda i: (0, 0))
    return pl.pallas_call(
        _ab_proj_kernel,
        grid=(n // tm,),
        in_specs=[row(D_MODEL), full(w16), row(1), full(invf)],
        out_specs=[row(HA)] * 7 + [row(LANES)] + [row(HA)] * 4,
        out_shape=[wide] * 7 + [jax.ShapeDtypeStruct((n, LANES), F32)] + [wide16] * 4,
        compiler_params=_cparams("parallel"),
        name="ab_project",
    )(x2d, w16, pos, invf)


def _fox_proj_kernel(x_ref, w_ref, bf_ref, q_ref, k_ref, v_ref, logf_ref, cum_ref,
                     k16_ref, v16_ref, carry_ref, *, tiles_per_seq):
    i = pl.program_id(0)
    xs = _lhs(x_ref[...], w_ref)
    q_ref[...] = _mm(xs, w_ref[:, 0:HC])
    k = _mm(xs, w_ref[:, HC:2 * HC])
    k_ref[...] = k
    k16_ref[...] = k.astype(BF16)
    v = _mm(xs, w_ref[:, 2 * HC:3 * HC])
    v_ref[...] = v
    v16_ref[...] = v.astype(BF16)
    f = _mm(xs, w_ref[:, 3 * HC:3 * HC + LANES])
    logf = _log_sigmoid(f + bf_ref[...])
    logf_ref[...] = logf

    @pl.when(i % tiles_per_seq == 0)
    def _():
        carry_ref[...] = jnp.zeros_like(carry_ref)

    tm = logf.shape[0]
    r = lax.broadcasted_iota(I32, (tm, tm), 0)
    cc = lax.broadcasted_iota(I32, (tm, tm), 1)
    tril = jnp.where(cc <= r, 1.0, 0.0).astype(BF16)
    hi, mid, lo = _split3(logf)
    cum = (_dot(tril, hi) + _dot(tril, mid) + _dot(tril, lo)) + carry_ref[...]
    cum_ref[...] = cum
    carry_ref[...] = cum[tm - 1:tm, :]


def _fox_project(x2d, w16, bf_pad, tm, rows_per_seq):
    n = x2d.shape[0]
    wide = jax.ShapeDtypeStruct((n, HC), F32)
    wide16 = jax.ShapeDtypeStruct((n, HC), BF16)
    small = jax.ShapeDtypeStruct((n, LANES), F32)
    row = lambda w: pl.BlockSpec((tm, w), lambda i: (i, 0))
    full = lambda a: pl.BlockSpec(a.shape, lambda i: (0, 0))
    return pl.pallas_call(
        functools.partial(_fox_proj_kernel, tiles_per_seq=rows_per_seq // tm),
        grid=(n // tm,),
        in_specs=[row(D_MODEL), full(w16), full(bf_pad)],
        out_specs=[row(HC)] * 3 + [row(LANES)] * 2 + [row(HC)] * 2,
        out_shape=[wide] * 3 + [small] * 2 + [wide16] * 2,
        scratch_shapes=[pltpu.VMEM((1, LANES), F32)],
        compiler_params=_cparams("arbitrary"),
        name="fox_project",
    )(x2d, w16, bf_pad)


def _out_ln_kernel(o_ref, w_ref, x_ref, g_ref, b_ref, y_ref):
    o = o_ref[...]
    m = _mm(_lhs(o, w_ref) if o.dtype == F32 else (o,), w_ref[...])
    y_ref[...] = _layer_norm(ALPHA * x_ref[...] + m, g_ref[...], b_ref[...])


def _out_proj_ln(o16, w16, x2d, g, b, tm):
    n, k = o16.shape
    row = lambda w: pl.BlockSpec((tm, w), lambda i: (i, 0))
    full = lambda a: pl.BlockSpec(a.shape, lambda i: (0, 0))
    return pl.pallas_call(
        _out_ln_kernel,
        grid=(n // tm,),
        in_specs=[row(k), full(w16), row(D_MODEL), full(g), full(b)],
        out_specs=row(D_MODEL),
        out_shape=jax.ShapeDtypeStruct((n, D_MODEL), F32),
        compiler_params=_cparams("parallel"),
        name="out_proj_ln",
    )(o16, w16, x2d, g, b)


def _softmax_init(m_scr, l_scr, acc_scr):
    m_scr[...] = jnp.full(m_scr.shape, NEG, F32)
    l_scr[...] = jnp.zeros(l_scr.shape, F32)
    acc_scr[...] = jnp.zeros(acc_scr.shape, F32)


def _softmax_step(h, s, vt, m_scr, l_scr, acc_scr):
    m_old = m_scr[h]
    m_new = jnp.maximum(m_old, jnp.max(s, axis=0, keepdims=True))
    alpha = jnp.exp(m_old - m_new)
    p = jnp.exp(s - m_new)
    l_scr[h] = alpha * l_scr[h] + jnp.sum(p, axis=0, keepdims=True)
    acc_scr[h] = alpha * acc_scr[h] + _dot(vt, p.astype(BF16))
    m_scr[h] = m_new


def _pair_masks(shape):
    lane = lax.broadcasted_iota(I32, shape, 1)
    return lane < HEAD_DIM


def _store_pair_queries(q, qm_scr, base):
    lo_head = _pair_masks(q.shape)
    qs = (q * QK_SCALE).astype(BF16)
    zero = jnp.zeros_like(qs)
    qm_scr[base] = jnp.where(lo_head, qs, zero)
    qm_scr[base + 1] = jnp.where(lo_head, zero, qs)


def _pair_output(hp, l_scr, acc_scr):
    a0 = acc_scr[2 * hp] / l_scr[2 * hp]
    a1 = acc_scr[2 * hp + 1] / l_scr[2 * hp + 1]
    first = lax.broadcasted_iota(I32, a0.shape, 0) < HEAD_DIM
    return jnp.where(first, a0, a1).T.astype(BF16)


def _key_query_index(tk, tq, k0, q0):
    kidx = k0 + lax.broadcasted_iota(I32, (tk, tq), 0)
    qidx = q0 + lax.broadcasted_iota(I32, (tk, tq), 1)
    return kidx, qidx


def _sortable_key(x):
    b = pltpu.bitcast(x, I32)
    return jnp.where(b < 0, INT_MIN - b, b)


def _count_keys(key_scr, n_tiles, pred):
    _, tk, tq = key_scr.shape

    def body(j, acc):
        for cidx in range(tk // 8):
            chunk = key_scr[j, cidx * 8:(cidx + 1) * 8, :]
            acc = acc + jnp.where(pred(chunk, j * tk + cidx * 8), 1, 0)
        return acc

    acc = lax.fori_loop(0, n_tiles, body, jnp.zeros((8, tq), I32))
    return jnp.sum(acc, axis=0, keepdims=True)


def _topk_threshold(key_scr, n_tiles, topk, key_bits):
    _, _, tq = key_scr.shape
    rows8 = lambda x: jnp.broadcast_to(x, (8, tq))

    def bit_body(i, t):
        cand = rows8(t + jnp.left_shift(jnp.int32(1), 31 - i))
        cnt = _count_keys(key_scr, n_tiles, lambda k, k0: k >= cand)
        return jnp.where(cnt >= topk, cand[0:1, :], t)

    t = lax.fori_loop(0, 32, bit_body, jnp.full((1, tq), INT_MIN, I32))
    t8 = rows8(t)
    n_gt = _count_keys(key_scr, n_tiles, lambda k, k0: k > t8)
    n_eq = _count_keys(key_scr, n_tiles, lambda k, k0: k == t8)
    need = topk - n_gt
    excess = jnp.max(jnp.where(n_eq > need, 1, 0)) > 0

    def tie_search():
        sub = lax.broadcasted_iota(I32, (8, tq), 0)

        def body(i, x):
            cand = rows8(x + jnp.left_shift(jnp.int32(1), key_bits - 1 - i))
            cnt = _count_keys(key_scr, n_tiles, lambda k, k0: (k == t8) & (sub + k0 < cand))
            return jnp.where(cnt < need, cand[0:1, :], x)

        return lax.fori_loop(0, key_bits, body, jnp.zeros((1, tq), I32))

    c = lax.cond(excess, tie_search, lambda: jnp.full((1, tq), 2 ** 31 - 1, I32))
    c = jnp.where(t == INT_MIN, -1, c)
    return t, c


def _dsa_prompt_kernel(qa_ref, qi_ref, wt_ref, kic_ref, k_ref, vt_ref, o_ref,
                       key_scr, qic_scr, qm_scr, m_scr, l_scr, acc_scr,
                       *, tq, tk, topk, key_bits):
    qt = pl.program_id(1)
    q0 = qt * tq
    n_tiles = (q0 + tq + tk - 1) // tk

    qi = qi_ref[...]
    for h in range(IDX_HEADS):
        hi, lo = _split2(qi[:, h * IDX_DIM:(h + 1) * IDX_DIM])
        qic_scr[h] = jnp.concatenate([hi, hi, lo, lo], axis=1)
    wt = wt_ref[...]

    def score_tile(j, diagonal):
        kc = kic_ref[pl.ds(j * tk, tk), :]
        sc = jnp.zeros((tk, tq), F32)
        for h in range(IDX_HEADS):
            d = _dot_nt(kc, qic_scr[h])
            sc = sc + wt[h:h + 1, :] * jnp.maximum(d, 0.0)
        key = _sortable_key(sc * IDX_SCALE)
        if diagonal:
            kidx, qidx = _key_query_index(tk, tq, j * tk, q0)
            key = jnp.where(kidx <= qidx, key, INT_MIN)
        key_scr[j] = key

    lax.fori_loop(0, n_tiles - 1, lambda j, _: (score_tile(j, False), 0)[1], 0)
    score_tile(n_tiles - 1, True)

    t, c = _topk_threshold(key_scr, n_tiles, topk, key_bits)

    qa = qa_ref[...]
    for hp in range(N_HEADS_A // 2):
        _store_pair_queries(qa[:, hp * LANES:(hp + 1) * LANES], qm_scr, 2 * hp)
    _softmax_init(m_scr, l_scr, acc_scr)

    def attn_tile(j, _):
        kt = key_scr[j]
        kidx = j * tk + lax.broadcasted_iota(I32, (tk, tq), 0)
        msk = (kt > t) | ((kt == t) & (kidx <= c))
        for hp in range(N_HEADS_A // 2):
            k = k_ref[pl.ds(j * tk, tk), hp * LANES:(hp + 1) * LANES]
            vt = vt_ref[hp, j]
            for h in (2 * hp, 2 * hp + 1):
                s = jnp.where(msk, _dot_nt(k, qm_scr[h]), NEG)
                _softmax_step(h, s, vt, m_scr, l_scr, acc_scr)
        return 0

    lax.fori_loop(0, n_tiles, attn_tile, 0)
    for hp in range(N_HEADS_A // 2):
        o_ref[:, hp * LANES:(hp + 1) * LANES] = _pair_output(hp, l_scr, acc_scr)


def _dsa_prompt(qa, qi, wt, ki_cat, k16, vt16, batch, seq, tq, tk):
    nq = seq // tq
    nk = seq // tk
    npair = N_HEADS_A // 2
    topk = min(DSA_TOPK, seq // 4)
    key_bits = max(1, (seq - 1).bit_length())
    once = pl.Buffered(1)
    qrow = lambda w: pl.BlockSpec((tq, w), lambda b, i: (b * nq + i, 0))
    return pl.pallas_call(
        functools.partial(_dsa_prompt_kernel, tq=tq, tk=tk, topk=topk, key_bits=key_bits),
        grid=(batch, nq),
        in_specs=[
            qrow(HA), qrow(HA),
            pl.BlockSpec((None, IDX_HEADS, tq), lambda b, i: (b, 0, i)),
            pl.BlockSpec((seq, 4 * IDX_DIM), lambda b, i: (b, 0), pipeline_mode=once),
            pl.BlockSpec((seq, HA), lambda b, i: (b, 0), pipeline_mode=once),
            pl.BlockSpec((None, npair, nk, LANES, tk), lambda b, i: (b, 0, 0, 0, 0), pipeline_mode=once),
        ],
        out_specs=qrow(HA),
        out_shape=jax.ShapeDtypeStruct((batch * seq, HA), BF16),
        scratch_shapes=[
            pltpu.VMEM((nk, tk, tq), I32),
            pltpu.VMEM((IDX_HEADS, tq, 4 * IDX_DIM), BF16),
            pltpu.VMEM((N_HEADS_A, tq, LANES), BF16),
            pltpu.VMEM((N_HEADS_A, 1, tq), F32),
            pltpu.VMEM((N_HEADS_A, 1, tq), F32),
            pltpu.VMEM((N_HEADS_A, LANES, tq), F32),
        ],
        compiler_params=_cparams("parallel", "parallel"),
        name="dsa_prompt",
    )(qa, qi, wt, ki_cat, k16, vt16)


def _block_mean_kernel(k_ref, o_ref):
    o_ref[...] = jnp.mean(k_ref[...], axis=0, keepdims=True)[None]


def _block_means(k2d):
    n = k2d.shape[0]
    nb = n // MOBA_BLOCK
    return pl.pallas_call(
        _block_mean_kernel,
        grid=(nb,),
        in_specs=[pl.BlockSpec((MOBA_BLOCK, HB), lambda i: (i, 0))],
        out_specs=pl.BlockSpec((1, 1, HB), lambda i: (i, 0, 0)),
        out_shape=jax.ShapeDtypeStruct((nb, 1, HB), F32),
        compiler_params=_cparams("parallel"),
        name="moba_block_means",
    )(k2d)


def _moba_prompt_kernel(q_ref, km_ref, k_ref, vt_ref, o_ref,
                        sel_scr, qm_scr, m_scr, l_scr, acc_scr, *, t):
    qt = pl.program_id(2)
    q = q_ref[...]
    nb = km_ref.shape[0]
    lo_head = _pair_masks(q.shape)
    km_hi, km_lo = _split2(km_ref[...])
    blk = lax.broadcasted_iota(I32, (nb, t), 0)
    for hh in range(2):
        qh = jnp.where(lo_head if hh == 0 else ~lo_head, q, 0.0)
        q_hi, q_lo = _split2(qh)
        g = _dot3(km_hi, km_lo, q_hi, q_lo, _dot_nt)
        g = jnp.where(blk < qt, g, -jnp.inf)
        sel = jnp.zeros((nb, t), F32)
        for _ in range(MOBA_TOPK):
            mx = jnp.max(g, axis=0, keepdims=True)
            first = jnp.min(jnp.where(g == mx, blk, nb), axis=0, keepdims=True)
            pick = (blk == first) & (mx > -jnp.inf)
            sel = jnp.where(pick, 1.0, sel)
            g = jnp.where(blk == first, -jnp.inf, g)
        sel_scr[hh] = sel
    _store_pair_queries(q, qm_scr, 0)
    _softmax_init(m_scr, l_scr, acc_scr)

    def past_tile(j, _):
        k = k_ref[pl.ds(j * t, t), :]
        vt = vt_ref[j]
        for hh in range(2):
            chosen = sel_scr[hh, pl.ds(j, 1), :] > 0.0
            s = jnp.where(chosen, _dot_nt(k, qm_scr[hh]), NEG)
            _softmax_step(hh, s, vt, m_scr, l_scr, acc_scr)
        return 0

    lax.fori_loop(0, qt, past_tile, 0)
    k = k_ref[pl.ds(qt * t, t), :]
    vt = vt_ref[qt]
    kidx, qidx = _key_query_index(t, t, 0, 0)
    for hh in range(2):
        s = jnp.where(kidx <= qidx, _dot_nt(k, qm_scr[hh]), NEG)
        _softmax_step(hh, s, vt, m_scr, l_scr, acc_scr)
    o_ref[...] = _pair_output(0, l_scr, acc_scr)


def _moba_prompt(q, kmean, k16, vt16, batch, seq):
    t = MOBA_T
    nq = seq // t
    nb = seq // MOBA_BLOCK
    npair = N_HEADS_B // 2
    return pl.pallas_call(
        functools.partial(_moba_prompt_kernel, t=t),
        grid=(batch, npair, nq),
        in_specs=[
            pl.BlockSpec((t, LANES), lambda b, p, i: (b * nq + i, p)),
            pl.BlockSpec((None, nb, LANES), lambda b, p, i: (b, 0, p)),
            pl.BlockSpec((seq, LANES), lambda b, p, i: (b, p)),
            pl.BlockSpec((None, None, nq, LANES, t), lambda b, p, i: (b, p, 0, 0, 0)),
        ],
        out_specs=pl.BlockSpec((t, LANES), lambda b, p, i: (b * nq + i, p)),
        out_shape=jax.ShapeDtypeStruct((batch * seq, HB), BF16),
        scratch_shapes=[
            pltpu.VMEM((2, nb, t), F32),
            pltpu.VMEM((2, t, LANES), BF16),
            pltpu.VMEM((2, 1, t), F32),
            pltpu.VMEM((2, 1, t), F32),
            pltpu.VMEM((2, LANES, t), F32),
        ],
        compiler_params=_cparams("parallel", "parallel", "parallel"),
        name="moba_prompt",
    )(q, kmean, k16, vt16)


def _fox_prompt_kernel(q_ref, cq_ref, ck_ref, k_ref, vt_ref, o_ref,
                       qm_scr, m_scr, l_scr, acc_scr, *, tq, tk):
    qt = pl.program_id(2)
    q0 = qt * tq
    _store_pair_queries(q_ref[...], qm_scr, 0)
    _softmax_init(m_scr, l_scr, acc_scr)
    cq = cq_ref[...]

    def tile(j, diagonal):
        k = k_ref[pl.ds(j * tk, tk), :]
        vt = vt_ref[j]
        ck = ck_ref[pl.ds(j * tk, tk), :]
        if diagonal:
            kidx, qidx = _key_query_index(tk, tq, j * tk, q0)
        logits = [_dot_nt(k, qm_scr[hh]) for hh in range(2)]
        for hh in range(2):
            s = logits[hh] + (cq[hh:hh + 1, :] - ck[:, hh:hh + 1])
            if diagonal:
                s = jnp.where(kidx <= qidx, s, NEG)
            _softmax_step(hh, s, vt, m_scr, l_scr, acc_scr)

    n_full = q0 // tk
    lax.fori_loop(0, n_full, lambda j, _: (tile(j, False), 0)[1], 0)
    tile(n_full, True)
    o_ref[...] = _pair_output(0, l_scr, acc_scr)


def _fox_prompt(q, cum_q, cum_k, k16, vt16, batch, seq, tq, tk):
    nq = seq // tq
    nk = seq // tk
    npair = N_HEADS_C // 2
    return pl.pallas_call(
        functools.partial(_fox_prompt_kernel, tq=tq, tk=tk),
        grid=(batch, npair, nq),
        in_specs=[
            pl.BlockSpec((tq, LANES), lambda b, p, i: (b * nq + i, p)),
            pl.BlockSpec((None, None, 2, tq), lambda b, p, i: (b, p, 0, i)),
            pl.BlockSpec((None, None, seq, 2), lambda b, p, i: (b, p, 0, 0)),
            pl.BlockSpec((seq, LANES), lambda b, p, i: (b, p)),
            pl.BlockSpec((None, None, nk, LANES, tk), lambda b, p, i: (b, p, 0, 0, 0)),
        ],
        out_specs=pl.BlockSpec((tq, LANES), lambda b, p, i: (b * nq + i, p)),
        out_shape=jax.ShapeDtypeStruct((batch * seq, HC), BF16),
        scratch_shapes=[
            pltpu.VMEM((2, tq, LANES), BF16),
            pltpu.VMEM((2, 1, tq), F32),
            pltpu.VMEM((2, 1, tq), F32),
            pltpu.VMEM((2, LANES, tq), F32),
        ],
        compiler_params=_cparams("parallel", "parallel", "parallel"),
        name="fox_prompt",
    )(q, cum_q, cum_k, k16, vt16)


def _router_kernel(x_ref, rwh_ref, rwl_ref, rb_ref, g_ref):
    x_hi, x_lo = _split2(x_ref[...])
    logits = _dot3(rwh_ref[...], rwl_ref[...], x_hi, x_lo, _dot_nt)
    scores = 1.0 / (1.0 + jnp.exp(-logits))
    biased = scores + rb_ref[...]
    rows = [biased[e:e + 1, :] for e in range(N_EXPERTS)]
    ninf = jnp.full_like(rows[0], -jnp.inf)

    def top2(vals):
        mx = functools.reduce(jnp.maximum, vals)
        picks1, found = [], jnp.zeros_like(mx) > 0
        for vv in vals:
            p = (vv == mx) & ~found
            found = found | p
            picks1.append(p)
        rest = [jnp.where(p, ninf, vv) for p, vv in zip(picks1, vals)]
        mx2 = functools.reduce(jnp.maximum, rest)
        picks2, found = [], jnp.zeros_like(mx) > 0
        for vv in rest:
            p = (vv == mx2) & ~found
            found = found | p
            picks2.append(p)
        return mx, mx2, picks1, picks2

    grp_score = []
    for gi in range(N_GROUPS):
        m1, m2, _, _ = top2(rows[gi * EXPERTS_PER_GROUP:(gi + 1) * EXPERTS_PER_GROUP])
        grp_score.append(m1 + m2)
    best = grp_score[0]
    g_sel = jnp.zeros_like(best, dtype=I32)
    for gi in range(1, N_GROUPS):
        better = grp_score[gi] > best
        best = jnp.where(better, grp_score[gi], best)
        g_sel = jnp.where(better, gi, g_sel)
    masked = [jnp.where(g_sel == e // EXPERTS_PER_GROUP, rows[e], ninf) for e in range(N_EXPERTS)]
    _, _, p1, p2 = top2(masked)
    zero = jnp.zeros_like(best)
    w1 = functools.reduce(jnp.add, [jnp.where(p1[e], scores[e:e + 1, :], zero) for e in range(N_EXPERTS)])
    w2 = functools.reduce(jnp.add, [jnp.where(p2[e], scores[e:e + 1, :], zero) for e in range(N_EXPERTS)])
    tot = w1 + w2
    for e in range(N_EXPERTS):
        g_ref[e:e + 1, :] = jnp.where(p1[e], w1 / tot, zero) + jnp.where(p2[e], w2 / tot, zero)


def _router(x2d, rwt_hi, rwt_lo, rb, tm):
    n = x2d.shape[0]
    full = lambda a: pl.BlockSpec(a.shape, lambda i: (0, 0))
    return pl.pallas_call(
        _router_kernel,
        grid=(n // tm,),
        in_specs=[pl.BlockSpec((tm, D_MODEL), lambda i: (i, 0)), full(rwt_hi), full(rwt_lo), full(rb)],
        out_specs=pl.BlockSpec((N_EXPERTS, tm), lambda i: (0, i)),
        out_shape=jax.ShapeDtypeStruct((N_EXPERTS, n), F32),
        compiler_params=_cparams("parallel"),
        name="moe_router",
    )(x2d, rwt_hi, rwt_lo, rb)


def _moe_kernel(x_ref, gates_ref, wg_ref, wu_ref, wd_ref, g_ref, b_ref, y_ref, xb_scr, acc_scr):
    e = pl.program_id(1)

    @pl.when(e == 0)
    def _():
        for i, part in enumerate(_lhs(x_ref[...], wg_ref)):
            xb_scr[i] = part
        acc_scr[...] = jnp.zeros_like(acc_scr)

    xs = tuple(xb_scr[i] for i in range(xb_scr.shape[0]))
    gate = _mm(xs, wg_ref[0])
    up = _mm(xs, wu_ref[0])
    h = gate * (1.0 / (1.0 + jnp.exp(-gate))) * up
    down = _mm(_lhs(h, wd_ref), wd_ref[0])
    gates = gates_ref[...]
    lane = lax.broadcasted_iota(I32, gates.shape, 1)
    w = jnp.sum(jnp.where(lane == e, gates, 0.0), axis=1, keepdims=True)
    acc_scr[...] += w * down

    @pl.when(e == pl.num_programs(1) - 1)
    def _():
        y_ref[...] = _layer_norm(ALPHA * x_ref[...] + acc_scr[...], g_ref[...], b_ref[...])


def _moe_ln(x2d, gates, wg16, wu16, wd16, g, b, tm):
    n = x2d.shape[0]
    full = lambda a: pl.BlockSpec(a.shape, lambda i, e: (0, 0))
    return pl.pallas_call(
        _moe_kernel,
        grid=(n // tm, N_EXPERTS),
        in_specs=[
            pl.BlockSpec((tm, D_MODEL), lambda i, e: (i, 0)),
            pl.BlockSpec((tm, N_EXPERTS), lambda i, e: (i, 0)),
            pl.BlockSpec((1, D_MODEL, D_EXPERT), lambda i, e: (e, 0, 0)),
            pl.BlockSpec((1, D_MODEL, D_EXPERT), lambda i, e: (e, 0, 0)),
            pl.BlockSpec((1, D_EXPERT, D_MODEL), lambda i, e: (e, 0, 0)),
            full(g), full(b),
        ],
        out_specs=pl.BlockSpec((tm, D_MODEL), lambda i, e: (i, 0)),
        out_shape=jax.ShapeDtypeStruct((n, D_MODEL), F32),
        scratch_shapes=[pltpu.VMEM((2 if wg16.dtype == F32 else 1, tm, D_MODEL), BF16),
                        pltpu.VMEM((tm, D_MODEL), F32)],
        compiler_params=_cparams("parallel", "arbitrary"),
        name="moe_experts_ln",
    )(x2d, gates, wg16, wu16, wd16, g, b)


def _page_view(cache):
    return jnp.moveaxis(cache, 1, -1)


def _one_key_page(col):
    shape = col.shape[:-1] + (PAGE_SIZE,)
    lane = lax.broadcasted_iota(I32, shape, len(shape) - 1)
    return jnp.where(lane == 0, col, 0.0)


def _new_key_bias(bias_new):
    lane = lax.broadcasted_iota(I32, (bias_new.shape[0], PAGE_SIZE), 1)
    return jnp.where(lane == 0, bias_new, NEG)


def _decode_logits(q_ref, k_refs, knew_ref, s_scr):
    n_heads = q_ref.shape[0]
    qb = jnp.broadcast_to(q_ref[...] * QK_SCALE, (n_heads, HEAD_DIM, PAGE_SIZE))
    for p, ref in enumerate(k_refs):
        s_scr[p] = jnp.sum(ref[...] * qb, axis=1)
    s_scr[len(k_refs)] = jnp.sum(_one_key_page(knew_ref[...]) * qb, axis=1)


def _decode_attend(s_scr, v_refs, vnew_ref, o_ref):
    s = s_scr[...]
    m = jnp.max(jnp.max(s, axis=0), axis=1, keepdims=True)
    s_scr[...] = jnp.exp(s - m)
    l = jnp.sum(jnp.sum(s_scr[...], axis=0), axis=1, keepdims=True)
    acc = _one_key_page(vnew_ref[...]) * s_scr[len(v_refs)][:, None, :]
    for p, ref in enumerate(v_refs):
        acc = acc + ref[...] * s_scr[p][:, None, :]
    o_ref[...] = jnp.sum(acc, axis=2, keepdims=True) / l[:, :, None]


def _dsa_sample_score_kernel(pt_ref, q_ref, w_ref, knew_ref, *rest, n_pages):
    page_refs = rest[:n_pages]
    s_ref = rest[n_pages]
    q = q_ref[...]
    q_hi, q_lo = _split2(q)
    w = w_ref[...]
    for p in range(n_pages):
        k_hi, k_lo = _split2(page_refs[p][...])
        d = _dot3(q_hi, q_lo, k_hi, k_lo, _dot)
        s_ref[p:p + 1, :] = jnp.sum(w * jnp.maximum(d, 0.0), axis=0, keepdims=True) * IDX_SCALE
    k_new = knew_ref[...][:, 0:IDX_DIM]
    d_new = jnp.sum(q * k_new, axis=1, keepdims=True)
    s_new = jnp.sum(w * jnp.maximum(d_new, 0.0), axis=0, keepdims=True) * IDX_SCALE
    lane = lax.broadcasted_iota(I32, (1, PAGE_SIZE), 1)
    s_ref[n_pages:n_pages + 1, :] = jnp.where(lane == 0, s_new, -jnp.inf)


def _dsa_sample_scores(page_table, q_i, w_i, tail, idx_view):
    nseq, n_pages = page_table.shape
    page = lambda p: pl.BlockSpec((None, IDX_DIM, PAGE_SIZE), lambda b, pt: (pt[b, p], 0, 0))
    grid_spec = pltpu.PrefetchScalarGridSpec(
        num_scalar_prefetch=1,
        grid=(nseq,),
        in_specs=[
            pl.BlockSpec((None, IDX_HEADS, IDX_DIM), lambda b, pt: (b, 0, 0)),
            pl.BlockSpec((None, IDX_HEADS, 1), lambda b, pt: (b, 0, 0)),
            pl.BlockSpec((None, 1, LANES), lambda b, pt: (b, 0, 0)),
        ] + [page(p) for p in range(n_pages)],
        out_specs=pl.BlockSpec((None, n_pages + 1, PAGE_SIZE), lambda b, pt: (b, 0, 0)),
    )
    return pl.pallas_call(
        functools.partial(_dsa_sample_score_kernel, n_pages=n_pages),
        grid_spec=grid_spec,
        out_shape=jax.ShapeDtypeStruct((nseq, n_pages + 1, PAGE_SIZE), F32),
        compiler_params=_cparams("parallel"),
        name="dsa_sample_scores",
    )(page_table, q_i, w_i, tail, *([idx_view] * n_pages))


def _select_kernel(s_ref, valid_ref, bias_ref, key_scr, *, topk, key_bits):
    n_tiles, tk, nseq = s_ref.shape
    for j in range(n_tiles):
        key_scr[j] = jnp.where(valid_ref[j] > 0.0, _sortable_key(s_ref[j]), INT_MIN)
    t, c = _topk_threshold(key_scr, n_tiles, topk, key_bits)
    for j in range(n_tiles):
        kt = key_scr[j]
        kidx = j * tk + lax.broadcasted_iota(I32, kt.shape, 0)
        bias_ref[j] = jnp.where((kt > t) | ((kt == t) & (kidx <= c)), 0.0, NEG)


def _select_topk_bias(scores_t, valid_t, topk):
    n_chunks = scores_t.shape[0]
    key_bits = max(1, (n_chunks * PAGE_SIZE - 1).bit_length())
    return pl.pallas_call(
        functools.partial(_select_kernel, topk=topk, key_bits=key_bits),
        out_shape=jax.ShapeDtypeStruct(scores_t.shape, F32),
        scratch_shapes=[pltpu.VMEM(scores_t.shape, I32)],
        compiler_params=pltpu.CompilerParams(vmem_limit_bytes=VMEM_LIMIT),
        name="dsa_sample_select",
    )(scores_t, valid_t)


def _decode_call(kernel_fn, name, page_table, n_heads, head_inputs, extra_inputs, extra_specs, k_view, v_view):
    nseq, n_pages = page_table.shape
    hd = pl.BlockSpec((None, n_heads, HEAD_DIM, 1), lambda b, pt: (b, 0, 0, 0))
    page = lambda p: pl.BlockSpec((None, n_heads, HEAD_DIM, PAGE_SIZE), lambda b, pt: (pt[b, p], 0, 0, 0))
    pages = [page(p) for p in range(n_pages)]
    grid_spec = pltpu.PrefetchScalarGridSpec(
        num_scalar_prefetch=1,
        grid=(nseq,),
        in_specs=[hd] * len(head_inputs) + list(extra_specs) + pages + pages,
        out_specs=hd,
        scratch_shapes=[pltpu.VMEM((n_pages + 1, n_heads, PAGE_SIZE), F32)],
    )
    return pl.pallas_call(
        functools.partial(kernel_fn, n_pages=n_pages),
        grid_spec=grid_spec,
        out_shape=jax.ShapeDtypeStruct((nseq, n_heads, HEAD_DIM, 1), F32),
        compiler_params=_cparams("parallel"),
        name=name,
    )(page_table, *head_inputs, *extra_inputs, *([k_view] * n_pages), *([v_view] * n_pages))


def _dsa_sample_attn_kernel(pt_ref, q_ref, knew_ref, vnew_ref, bias_ref, *rest, n_pages):
    k_refs, v_refs = rest[:n_pages], rest[n_pages:2 * n_pages]
    o_ref, s_scr = rest[2 * n_pages], rest[2 * n_pages + 1]
    _decode_logits(q_ref, k_refs, knew_ref, s_scr)
    s_scr[...] = s_scr[...] + bias_ref[...][:, None, :]
    _decode_attend(s_scr, v_refs, vnew_ref, o_ref)


def _dsa_sample_attn(page_table, q, k_new, v_new, bias, k_view, v_view):
    n_pages = page_table.shape[1]
    spec = pl.BlockSpec((None, n_pages + 1, PAGE_SIZE), lambda b, pt: (b, 0, 0))
    return _decode_call(_dsa_sample_attn_kernel, "dsa_sample_attn", page_table, N_HEADS_A,
                        (q, k_new, v_new), (bias,), (spec,), k_view, v_view)


def _moba_sample_kernel(pt_ref, q_ref, knew_ref, vnew_ref, *rest, n_pages):
    k_refs, v_refs = rest[:n_pages], rest[n_pages:2 * n_pages]
    o_ref, s_scr = rest[2 * n_pages], rest[2 * n_pages + 1]
    ppb = MOBA_BLOCK // PAGE_SIZE
    n_blocks = n_pages // ppb
    q = q_ref[...]
    gates = []
    for n in range(n_blocks):
        ksum = functools.reduce(jnp.add, [k_refs[n * ppb + i][...] for i in range(ppb)])
        k_mean = jnp.sum(ksum, axis=2, keepdims=True) / float(MOBA_BLOCK)
        gates.append(jnp.sum(q * k_mean, axis=1))
    chosen = [jnp.zeros_like(gates[0]) > 0 for _ in range(n_blocks)]
    for _ in range(min(MOBA_TOPK, n_blocks + 1)):
        mx = functools.reduce(jnp.maximum, gates)
        found = jnp.zeros_like(mx) > 0
        for n in range(n_blocks):
            pick = (gates[n] == mx) & ~found & (mx > -jnp.inf)
            found = found | pick
            chosen[n] = chosen[n] | pick
            gates[n] = jnp.where(pick, -jnp.inf, gates[n])
    _decode_logits(q_ref, k_refs, knew_ref, s_scr)
    for p in range(n_pages):
        s_scr[p] = s_scr[p] + jnp.where(chosen[p // ppb], 0.0, NEG)
    s_scr[n_pages] = s_scr[n_pages] + _new_key_bias(jnp.zeros((1, 1), F32))
    _decode_attend(s_scr, v_refs, vnew_ref, o_ref)


def _moba_sample(page_table, q, k_new, v_new, k_view, v_view):
    return _decode_call(_moba_sample_kernel, "moba_sample", page_table, N_HEADS_B,
                        (q, k_new, v_new), (), (), k_view, v_view)


def _fox_sample_kernel(pt_ref, q_ref, knew_ref, vnew_ref, lnew_ref, *rest, n_pages):
    l_refs = rest[:n_pages]
    k_refs, v_refs = rest[n_pages:2 * n_pages], rest[2 * n_pages:3 * n_pages]
    o_ref, s_scr = rest[3 * n_pages], rest[3 * n_pages + 1]
    r = lax.broadcasted_iota(I32, (PAGE_SIZE, PAGE_SIZE), 0)
    cc = lax.broadcasted_iota(I32, (PAGE_SIZE, PAGE_SIZE), 1)
    upper = jnp.where(r <= cc, 1.0, 0.0).astype(BF16)
    carry = jnp.zeros((N_HEADS_C, 1), F32)
    cums = []
    for p in range(n_pages):
        hi, mid, lo = _split3(l_refs[p][...])
        cum = (_dot(hi, upper) + _dot(mid, upper) + _dot(lo, upper)) + carry
        cums.append(cum)
        carry = cum[:, PAGE_SIZE - 1:PAGE_SIZE]
    cum_q = carry + lnew_ref[...]
    _decode_logits(q_ref, k_refs, knew_ref, s_scr)
    for p in range(n_pages):
        s_scr[p] = s_scr[p] + (cum_q - cums[p])
    s_scr[n_pages] = s_scr[n_pages] + _new_key_bias(cum_q - cum_q)
    _decode_attend(s_scr, v_refs, vnew_ref, o_ref)


def _fox_sample(page_table, q, k_new, v_new, logf_new, logf_view, k_view, v_view):
    n_pages = page_table.shape[1]
    lnew = pl.BlockSpec((None, N_HEADS_C, 1), lambda b, pt: (b, 0, 0))
    lpage = lambda p: pl.BlockSpec((None, N_HEADS_C, PAGE_SIZE), lambda b, pt: (pt[b, p], 0, 0))
    return _decode_call(_fox_sample_kernel, "fox_sample", page_table, N_HEADS_C,
                        (q, k_new, v_new), (logf_new,) + (logf_view,) * n_pages,
                        (lnew,) + tuple(lpage(p) for p in range(n_pages)), k_view, v_view)


def _pad_cols(w, width):
    return jnp.pad(w, ((0, 0), (0, width - w.shape[1])))


def _rope_inv_freq():
    half = HEAD_DIM // 8
    inv = ROPE_THETA ** (-jnp.arange(half, dtype=F32) / half)
    per_head = jnp.concatenate([inv, inv, jnp.zeros((HEAD_DIM - 2 * half,), F32)])
    return jnp.tile(per_head, LANES // HEAD_DIM)[None, :]


def _transposed_values(v16, batch, seq, tk):
    npair = v16.shape[1] // LANES
    return v16.reshape(batch, seq // tk, tk, npair, LANES).transpose(0, 3, 1, 4, 2)


def _ab_layer_prompt(x2d, w16, w_out16, invf, g, b, batch, seq):
    pos = jnp.tile(jnp.arange(seq, dtype=F32), batch)[:, None]
    qa, ka, va, qb, kb, vb, qi, tail, ka16, va16, kb16, vb16 = _ab_project(x2d, w16, pos, invf, PROJ_TM)
    k_idx = tail[:, :IDX_DIM]
    ki_hi, ki_lo = _split2(k_idx)
    ki_cat = jnp.concatenate([ki_hi, ki_lo, ki_hi, ki_lo], axis=1)
    wt = tail[:, IDX_DIM:IDX_DIM + IDX_HEADS].reshape(batch, seq, IDX_HEADS).transpose(0, 2, 1)
    o_a = _dsa_prompt(qa, qi, wt, ki_cat, ka16, _transposed_values(va16, batch, seq, DSA_TK),
                      batch, seq, DSA_TQ, DSA_TK)
    kmean = _block_means(kb).reshape(batch, seq // MOBA_BLOCK, HB)
    o_b = _moba_prompt(qb, kmean, kb16, _transposed_values(vb16, batch, seq, MOBA_T), batch, seq)
    o = jnp.concatenate([o_a, o_b], axis=1)
    y = _out_proj_ln(o, w_out16, x2d, g, b, PROJ_TM)
    return y, (ka, va, k_idx, kb, vb)


def _ab_layer_sample(x2d, w16, w_out16, invf, g, b, past_len, page_table,
                     cache_a_k, cache_a_v, cache_a_idx, cache_b_k, cache_b_v):
    nseq = x2d.shape[0]
    n_pages = page_table.shape[1]
    pos = jnp.full((nseq, 1), past_len, F32)
    qa, ka, va, qb, kb, vb, qi, tail, _, _, _, _ = _ab_project(x2d, w16, pos, invf, nseq)
    k_idx = tail[:, :IDX_DIM]
    w_i = tail[:, IDX_DIM:IDX_DIM + IDX_HEADS]
    col = lambda a: a.reshape(nseq, -1, HEAD_DIM, 1)
    scores = _dsa_sample_scores(page_table, qi.reshape(nseq, IDX_HEADS, IDX_DIM), w_i.reshape(nseq, IDX_HEADS, 1),
                                tail.reshape(nseq, 1, LANES), _page_view(cache_a_idx))
    n_keys = n_pages * PAGE_SIZE + 1
    valid = (jnp.arange((n_pages + 1) * PAGE_SIZE) < n_keys).astype(F32)
    valid_t = jnp.broadcast_to(valid.reshape(n_pages + 1, PAGE_SIZE, 1), (n_pages + 1, PAGE_SIZE, nseq))
    bias_t = _select_topk_bias(scores.transpose(1, 2, 0), valid_t, min(DSA_TOPK, n_keys // 4))
    o_a = _dsa_sample_attn(page_table, col(qa), col(ka), col(va), bias_t.transpose(2, 0, 1),
                           _page_view(cache_a_k), _page_view(cache_a_v))
    o_b = _moba_sample(page_table, col(qb), col(kb), col(vb), _page_view(cache_b_k), _page_view(cache_b_v))
    o = jnp.concatenate([o_a.reshape(nseq, HA), o_b.reshape(nseq, HB)], axis=1)
    y = _out_proj_ln(o, w_out16, x2d, g, b, nseq)
    return y, (ka, va, k_idx, kb, vb)


def _fox_layer_prompt(x2d, w16, bf_pad, w_out16, g, b, batch, seq):
    q, k, v, logf, cum, k16, v16 = _fox_project(x2d, w16, bf_pad, PROJ_TM, seq)
    npair = N_HEADS_C // 2
    cum_k = cum[:, :N_HEADS_C].reshape(batch, seq, npair, 2).transpose(0, 2, 1, 3)
    cum_q = cum_k.transpose(0, 1, 3, 2)
    o = _fox_prompt(q, cum_q, cum_k, k16, _transposed_values(v16, batch, seq, FOX_TK),
                    batch, seq, FOX_TQ, FOX_TK)
    y = _out_proj_ln(o, w_out16, x2d, g, b, PROJ_TM)
    return y, (k, v, logf[:, :N_HEADS_C])


def _fox_layer_sample(x2d, w16, bf_pad, w_out16, g, b, page_table, cache_c_k, cache_c_v, cache_c_logf):
    nseq = x2d.shape[0]
    q, k, v, logf, _, _, _ = _fox_project(x2d, w16, bf_pad, nseq, nseq)
    col = lambda a: a.reshape(nseq, N_HEADS_C, HEAD_DIM, 1)
    o = _fox_sample(page_table, col(q), col(k), col(v), logf[:, :N_HEADS_C].reshape(nseq, N_HEADS_C, 1),
                    _page_view(cache_c_logf), _page_view(cache_c_k), _page_view(cache_c_v))
    y = _out_proj_ln(o.reshape(nseq, HC), w_out16, x2d, g, b, nseq)
    return y, (k, v, logf[:, :N_HEADS_C])


def _moe_layer(x2d, rwt_hi, rwt_lo, rb, wg16, wu16, wd16, g, b, tm):
    gates_t = _router(x2d, rwt_hi, rwt_lo, rb, tm)
    return _moe_ln(x2d, gates_t.T, wg16, wu16, wd16, g, b, tm)


def kernel(x_prompt, x_sample, cache_a_k, cache_a_v, cache_a_idx, cache_b_k, cache_b_v, cache_c_k, cache_c_v, cache_c_logf, page_table, w_in_ab, w_out_ab, w_in_fox, b_forget, w_out_fox, ln_mix_g, ln_mix_b, ln_ffn_g, ln_ffn_b, router_w, router_bias, exp_w_gate, exp_w_up, exp_w_down):
    batch, seq, _ = x_prompt.shape
    nseq = x_sample.shape[0]
    past_len = page_table.shape[1] * PAGE_SIZE

    w_ab32 = _pad_cols(w_in_ab, 7 * HA + LANES)
    w_fox32 = _pad_cols(w_in_fox, 3 * HC + LANES)
    w_ab16, w_out_ab16 = w_ab32.astype(BF16), w_out_ab.astype(BF16)
    w_fox16, w_out_fox16 = w_fox32.astype(BF16), w_out_fox.astype(BF16)
    bf_pad = jnp.pad(b_forget, (0, LANES - N_HEADS_C))[None, :]
    invf = _rope_inv_freq()
    rwt_hi, rwt_lo = _split2(router_w.T)
    rb = router_bias[:, None]
    wg16, wu16, wd16 = exp_w_gate.astype(BF16), exp_w_up.astype(BF16), exp_w_down.astype(BF16)
    row = lambda a, i: a[i][None, :]

    xp = x_prompt.reshape(batch * seq, D_MODEL)
    xs = x_sample.reshape(nseq, D_MODEL)

    xp, (pa_k, pa_v, pa_idx, pb_k, pb_v) = _ab_layer_prompt(
        xp, w_ab16, w_out_ab16, invf, row(ln_mix_g, 0), row(ln_mix_b, 0), batch, seq)
    xs, (sa_k, sa_v, sa_idx, sb_k, sb_v) = _ab_layer_sample(
        xs, w_ab32, w_out_ab, invf, row(ln_mix_g, 0), row(ln_mix_b, 0), past_len, page_table,
        cache_a_k, cache_a_v, cache_a_idx, cache_b_k, cache_b_v)
    xp = _moe_layer(xp, rwt_hi, rwt_lo, rb, wg16[0], wu16[0], wd16[0], row(ln_ffn_g, 0), row(ln_ffn_b, 0), MOE_TM)
    xs = _moe_layer(xs, rwt_hi, rwt_lo, rb, exp_w_gate[0], exp_w_up[0], exp_w_down[0],
                    row(ln_ffn_g, 0), row(ln_ffn_b, 0), nseq)

    xp, (pc_k, pc_v, pc_logf) = _fox_layer_prompt(
        xp, w_fox16, bf_pad, w_out_fox16, row(ln_mix_g, 1), row(ln_mix_b, 1), batch, seq)
    xs, (sc_k, sc_v, sc_logf) = _fox_layer_sample(
        xs, w_fox32, bf_pad, w_out_fox, row(ln_mix_g, 1), row(ln_mix_b, 1), page_table,
        cache_c_k, cache_c_v, cache_c_logf)
    xp = _moe_layer(xp, rwt_hi, rwt_lo, rb, wg16[1], wu16[1], wd16[1], row(ln_ffn_g, 1), row(ln_ffn_b, 1), MOE_TM)
    xs = _moe_layer(xs, rwt_hi, rwt_lo, rb, exp_w_gate[1], exp_w_up[1], exp_w_down[1],
                    row(ln_ffn_g, 1), row(ln_ffn_b, 1), nseq)

    hd = lambda a, nh, lead: a.reshape(*lead, nh, HEAD_DIM)
    lp, ls = (batch, seq), (nseq, 1)
    return (xp.reshape(batch, seq, D_MODEL), xs.reshape(nseq, 1, D_MODEL),
            hd(pa_k, N_HEADS_A, lp), hd(pa_v, N_HEADS_A, lp), pa_idx.reshape(batch, seq, IDX_DIM),
            hd(pb_k, N_HEADS_B, lp), hd(pb_v, N_HEADS_B, lp),
            hd(pc_k, N_HEADS_C, lp), hd(pc_v, N_HEADS_C, lp), pc_logf.reshape(batch, seq, N_HEADS_C),
            hd(sa_k, N_HEADS_A, ls), hd(sa_v, N_HEADS_A, ls), sa_idx.reshape(nseq, 1, IDX_DIM),
            hd(sb_k, N_HEADS_B, ls), hd(sb_v, N_HEADS_B, ls),
            hd(sc_k, N_HEADS_C, ls), hd(sc_v, N_HEADS_C, ls), sc_logf.reshape(nseq, 1, N_HEADS_C))
```

```python
import functools

import jax
import jax.numpy as jnp
from jax import lax
from jax.experimental import pallas as pl
from jax.experimental.pallas import tpu as pltpu

F32 = jnp.float32
BF16 = jnp.bfloat16
I32 = jnp.int32

D_MODEL = 1024
DEPTH = 2
PAGE_SIZE = 128
HEAD_DIM = 64
N_HEADS_A = 8
N_HEADS_B = 8
N_HEADS_C = 16
IDX_HEADS = 8
IDX_DIM = 64
DSA_TOPK = 256
MOBA_BLOCK = 256
MOBA_TOPK = 3
ROPE_THETA = 500000.0
N_EXPERTS = 16
N_GROUPS = 4
EXPERTS_PER_GROUP = N_EXPERTS // N_GROUPS
D_EXPERT = 512
ALPHA = (2 * DEPTH) ** 0.25
LN_EPS = 1e-5
HA = N_HEADS_A * HEAD_DIM
HB = N_HEADS_B * HEAD_DIM
HC = N_HEADS_C * HEAD_DIM
QK_SCALE = HEAD_DIM ** -0.5
IDX_SCALE = IDX_DIM ** -0.5

LANES = 128
NEG = -1e30
INT_MIN = -2 ** 31
VMEM_LIMIT = 56 * 2 ** 20

PROJ_TM = 256
DSA_TQ = 256
DSA_TK = 512
MOBA_T = MOBA_BLOCK
FOX_TQ = 256
FOX_TK = 1024
MOE_TM = 1024


def _cparams(*sem):
    return pltpu.CompilerParams(dimension_semantics=sem, vmem_limit_bytes=VMEM_LIMIT)


def _dot(a, b):
    return jnp.dot(a, b, preferred_element_type=F32)


def _dot_nt(a, b):
    return lax.dot_general(a, b, (((1,), (1,)), ((), ())), preferred_element_type=F32)


def _split2(x):
    hi = x.astype(BF16)
    lo = (x - hi.astype(F32)).astype(BF16)
    return hi, lo


def _split3(x):
    hi = x.astype(BF16)
    r = x - hi.astype(F32)
    mid = r.astype(BF16)
    lo = (r - mid.astype(F32)).astype(BF16)
    return hi, mid, lo


def _dot3(a_hi, a_lo, b_hi, b_lo, dot):
    return dot(a_hi, b_hi) + (dot(a_hi, b_lo) + dot(a_lo, b_hi))


def _lhs(x, w_ref):
    return _split2(x) if w_ref.dtype == F32 else (x.astype(BF16),)


def _mm(xs, w):
    if len(xs) == 1:
        return _dot(xs[0], w)
    w_hi, w_lo = _split2(w)
    return _dot3(xs[0], xs[1], w_hi, w_lo, _dot)


def _layer_norm(z, g, b):
    mu = jnp.mean(z, axis=-1, keepdims=True)
    d = z - mu
    var = jnp.mean(d * d, axis=-1, keepdims=True)
    return d * lax.rsqrt(var + LN_EPS) * g + b


def _log_sigmoid(z):
    return -(jnp.maximum(-z, 0.0) + jnp.log1p(jnp.exp(-jnp.abs(z))))


def _rotary_tables(pos_ref, invf_ref):
    ang = pos_ref[...] * invf_ref[...]
    c = jnp.cos(ang)
    s = jnp.sin(ang)
    f = lax.broadcasted_iota(I32, ang.shape, 1) % HEAD_DIM
    s_up = jnp.where(f < 8, -s, 0.0)
    s_dn = jnp.where(f >= 8, s, 0.0)
    return c, s_up, s_dn


def _rotate(h, c, s_up, s_dn):
    outs = []
    for j in range(h.shape[1] // LANES):
        hc = h[:, j * LANES:(j + 1) * LANES]
        outs.append(hc * c + pltpu.roll(hc, LANES - 8, 1) * s_up + pltpu.roll(hc, 8, 1) * s_dn)
    return outs[0] if len(outs) == 1 else jnp.concatenate(outs, axis=1)


def _ab_proj_kernel(x_ref, w_ref, pos_ref, invf_ref,
                    qa_ref, ka_ref, va_ref, qb_ref, kb_ref, vb_ref, qi_ref, tail_ref,
                    ka16_ref, va16_ref, kb16_ref, vb16_ref, kic_ref):
    xs = _lhs(x_ref[...], w_ref)
    c, s_up, s_dn = _rotary_tables(pos_ref, invf_ref)

    def seg(j, width=HA):
        return _mm(xs, w_ref[:, j * HA:j * HA + width])

    qa_ref[...] = _rotate(seg(0), c, s_up, s_dn)
    ka = _rotate(seg(1), c, s_up, s_dn)
    ka_ref[...] = ka
    ka16_ref[...] = ka.astype(BF16)
    va = seg(2)
    va_ref[...] = va
    va16_ref[...] = va.astype(BF16)
    qb_ref[...] = _rotate(seg(3), c, s_up, s_dn)
    kb = _rotate(seg(4), c, s_up, s_dn)
    kb_ref[...] = kb
    kb16_ref[...] = kb.astype(BF16)
    vb = seg(5)
    vb_ref[...] = vb
    vb16_ref[...] = vb.astype(BF16)
    qi_ref[...] = _rotate(seg(6), c, s_up, s_dn)
    t = seg(7, LANES)
    lane = lax.broadcasted_iota(I32, t.shape, 1)
    is_key = lane < IDX_DIM
    ct = jnp.where(is_key, c, IDX_HEADS ** -0.5)
    tail = (t * ct + pltpu.roll(t, LANES - 8, 1) * jnp.where(is_key, s_up, 0.0)
            + pltpu.roll(t, 8, 1) * jnp.where(is_key, s_dn, 0.0))
    tail_ref[...] = tail
    ki_hi, ki_lo = _split2(tail[:, 0:IDX_DIM])
    kic_ref[...] = jnp.concatenate([ki_hi, ki_lo, ki_hi, ki_lo], axis=1)


def _ab_project(x2d, w16, pos, invf, tm):
    n = x2d.shape[0]
    wide = jax.ShapeDtypeStruct((n, HA), F32)
    wide16 = jax.ShapeDtypeStruct((n, HA), BF16)
    row = lambda w: pl.BlockSpec((tm, w), lambda i: (i, 0))
    full = lambda a: pl.BlockSpec(a.shape, lambda i: (0, 0))
    return pl.pallas_call(
        _ab_proj_kernel,
        grid=(n // tm,),
        in_specs=[row(D_MODEL), full(w16), row(1), full(invf)],
        out_specs=[row(HA)] * 7 + [row(LANES)] + [row(HA)] * 4 + [row(4 * IDX_DIM)],
        out_shape=([wide] * 7 + [jax.ShapeDtypeStruct((n, LANES), F32)] + [wide16] * 4
                   + [jax.ShapeDtypeStruct((n, 4 * IDX_DIM), BF16)]),
        compiler_params=_cparams("parallel"),
        name="ab_project",
    )(x2d, w16, pos, invf)


def _fox_proj_kernel(x_ref, w_ref, bf_ref, q_ref, k_ref, v_ref, logf_ref, cum_ref,
                     k16_ref, v16_ref, carry_ref, *, tiles_per_seq):
    i = pl.program_id(0)
    xs = _lhs(x_ref[...], w_ref)
    q_ref[...] = _mm(xs, w_ref[:, 0:HC])
    k = _mm(xs, w_ref[:, HC:2 * HC])
    k_ref[...] = k
    k16_ref[...] = k.astype(BF16)
    v = _mm(xs, w_ref[:, 2 * HC:3 * HC])
    v_ref[...] = v
    v16_ref[...] = v.astype(BF16)
    f = _mm(xs, w_ref[:, 3 * HC:3 * HC + LANES])
    logf = _log_sigmoid(f + bf_ref[...])
    logf_ref[...] = logf

    @pl.when(i % tiles_per_seq == 0)
    def _():
        carry_ref[...] = jnp.zeros_like(carry_ref)

    tm = logf.shape[0]
    r = lax.broadcasted_iota(I32, (tm, tm), 0)
    cc = lax.broadcasted_iota(I32, (tm, tm), 1)
    tril = jnp.where(cc <= r, 1.0, 0.0).astype(BF16)
    hi, mid, lo = _split3(logf)
    cum = (_dot(tril, hi) + _dot(tril, mid) + _dot(tril, lo)) + carry_ref[...]
    cum_ref[...] = cum
    carry_ref[...] = cum[tm - 1:tm, :]


def _fox_project(x2d, w16, bf_pad, tm, rows_per_seq):
    n = x2d.shape[0]
    wide = jax.ShapeDtypeStruct((n, HC), F32)
    wide16 = jax.ShapeDtypeStruct((n, HC), BF16)
    small = jax.ShapeDtypeStruct((n, LANES), F32)
    row = lambda w: pl.BlockSpec((tm, w), lambda i: (i, 0))
    full = lambda a: pl.BlockSpec(a.shape, lambda i: (0, 0))
    return pl.pallas_call(
        functools.partial(_fox_proj_kernel, tiles_per_seq=rows_per_seq // tm),
        grid=(n // tm,),
        in_specs=[row(D_MODEL), full(w16), full(bf_pad)],
        out_specs=[row(HC)] * 3 + [row(LANES)] * 2 + [row(HC)] * 2,
        out_shape=[wide] * 3 + [small] * 2 + [wide16] * 2,
        scratch_shapes=[pltpu.VMEM((1, LANES), F32)],
        compiler_params=_cparams("arbitrary"),
        name="fox_project",
    )(x2d, w16, bf_pad)


def _out_ln_kernel(o_ref, w_ref, x_ref, g_ref, b_ref, y_ref):
    o = o_ref[...]
    m = _mm(_lhs(o, w_ref) if o.dtype == F32 else (o,), w_ref[...])
    y_ref[...] = _layer_norm(ALPHA * x_ref[...] + m, g_ref[...], b_ref[...])


def _out_proj_ln(o16, w16, x2d, g, b, tm):
    n, k = o16.shape
    row = lambda w: pl.BlockSpec((tm, w), lambda i: (i, 0))
    full = lambda a: pl.BlockSpec(a.shape, lambda i: (0, 0))
    return pl.pallas_call(
        _out_ln_kernel,
        grid=(n // tm,),
        in_specs=[row(k), full(w16), row(D_MODEL), full(g), full(b)],
        out_specs=row(D_MODEL),
        out_shape=jax.ShapeDtypeStruct((n, D_MODEL), F32),
        compiler_params=_cparams("parallel"),
        name="out_proj_ln",
    )(o16, w16, x2d, g, b)


def _softmax_init(m_scr, l_scr, acc_scr):
    m_scr[...] = jnp.full(m_scr.shape, NEG, F32)
    l_scr[...] = jnp.zeros(l_scr.shape, F32)
    acc_scr[...] = jnp.zeros(acc_scr.shape, F32)


def _softmax_step(h, s, vt, m_scr, l_scr, acc_scr):
    m_old = m_scr[h]
    m_new = jnp.maximum(m_old, jnp.max(s, axis=0, keepdims=True))
    alpha = jnp.exp(m_old - m_new)
    p = jnp.exp(s - m_new)
    l_scr[h] = alpha * l_scr[h] + jnp.sum(p, axis=0, keepdims=True)
    acc_scr[h] = alpha * acc_scr[h] + _dot(vt, p.astype(BF16))
    m_scr[h] = m_new


def _pair_masks(shape):
    lane = lax.broadcasted_iota(I32, shape, 1)
    return lane < HEAD_DIM


def _store_pair_queries(q, qm_scr, base):
    lo_head = _pair_masks(q.shape)
    qs = (q * QK_SCALE).astype(BF16)
    zero = jnp.zeros_like(qs)
    qm_scr[base] = jnp.where(lo_head, qs, zero)
    qm_scr[base + 1] = jnp.where(lo_head, zero, qs)


def _pair_output(hp, l_scr, acc_scr):
    a0 = acc_scr[2 * hp] / l_scr[2 * hp]
    a1 = acc_scr[2 * hp + 1] / l_scr[2 * hp + 1]
    first = lax.broadcasted_iota(I32, a0.shape, 0) < HEAD_DIM
    return jnp.where(first, a0, a1).T.astype(BF16)


def _key_query_index(tk, tq, k0, q0):
    kidx = k0 + lax.broadcasted_iota(I32, (tk, tq), 0)
    qidx = q0 + lax.broadcasted_iota(I32, (tk, tq), 1)
    return kidx, qidx


def _sortable_key(x):
    b = pltpu.bitcast(x, I32)
    return jnp.where(b < 0, INT_MIN - b, b)


def _count_keys(key_scr, n_tiles, pred):
    _, tk, tq = key_scr.shape

    def body(j, acc):
        for cidx in range(tk // 8):
            chunk = key_scr[j, cidx * 8:(cidx + 1) * 8, :]
            acc = acc + jnp.where(pred(chunk, j * tk + cidx * 8), 1, 0)
        return acc

    acc = lax.fori_loop(0, n_tiles, body, jnp.zeros((8, tq), I32))
    return jnp.sum(acc, axis=0, keepdims=True)


def _topk_threshold(key_scr, n_tiles, topk, key_bits):
    _, _, tq = key_scr.shape
    rows8 = lambda x: jnp.broadcast_to(x, (8, tq))

    def bit_body(i, t):
        cand = rows8(t + jnp.left_shift(jnp.int32(1), 31 - i))
        cnt = _count_keys(key_scr, n_tiles, lambda k, k0: k >= cand)
        return jnp.where(cnt >= topk, cand[0:1, :], t)

    t = lax.fori_loop(0, 32, bit_body, jnp.full((1, tq), INT_MIN, I32))
    t8 = rows8(t)
    n_gt = _count_keys(key_scr, n_tiles, lambda k, k0: k > t8)
    n_eq = _count_keys(key_scr, n_tiles, lambda k, k0: k == t8)
    need = topk - n_gt
    excess = jnp.max(jnp.where(n_eq > need, 1, 0)) > 0

    def tie_search():
        sub = lax.broadcasted_iota(I32, (8, tq), 0)

        def body(i, x):
            cand = rows8(x + jnp.left_shift(jnp.int32(1), key_bits - 1 - i))
            cnt = _count_keys(key_scr, n_tiles, lambda k, k0: (k == t8) & (sub + k0 < cand))
            return jnp.where(cnt < need, cand[0:1, :], x)

        return lax.fori_loop(0, key_bits, body, jnp.zeros((1, tq), I32))

    c = lax.cond(excess, tie_search, lambda: jnp.full((1, tq), 2 ** 31 - 1, I32))
    c = jnp.where(t == INT_MIN, -1, c)
    return t, c


def _dsa_prompt_kernel(qa_ref, qi_ref, wt_ref, kic_ref, k_ref, vt_ref, o_ref,
                       key_scr, qic_scr, qm_scr, m_scr, l_scr, acc_scr,
                       *, tq, tk, topk, key_bits):
    qt = pl.program_id(1)
    q0 = qt * tq
    n_tiles = (q0 + tq + tk - 1) // tk

    qi = qi_ref[...]
    for h in range(IDX_HEADS):
        hi, lo = _split2(qi[:, h * IDX_DIM:(h + 1) * IDX_DIM])
        qic_scr[h] = jnp.concatenate([hi, hi, lo, lo], axis=1)
    wt = wt_ref[...]

    def score_tile(j, diagonal):
        kc = kic_ref[pl.ds(j * tk, tk), :]
        sc = jnp.zeros((tk, tq), F32)
        for h in range(IDX_HEADS):
            d = _dot_nt(kc, qic_scr[h])
            sc = sc + wt[h:h + 1, :] * jnp.maximum(d, 0.0)
        key = _sortable_key(sc * IDX_SCALE)
        if diagonal:
            kidx, qidx = _key_query_index(tk, tq, j * tk, q0)
            key = jnp.where(kidx <= qidx, key, INT_MIN)
        key_scr[j] = key

    lax.fori_loop(0, n_tiles - 1, lambda j, _: (score_tile(j, False), 0)[1], 0)
    score_tile(n_tiles - 1, True)

    t, c = _topk_threshold(key_scr, n_tiles, topk, key_bits)

    qa = qa_ref[...]
    for hp in range(N_HEADS_A // 2):
        _store_pair_queries(qa[:, hp * LANES:(hp + 1) * LANES], qm_scr, 2 * hp)
    _softmax_init(m_scr, l_scr, acc_scr)

    def attn_tile(j, _):
        kt = key_scr[j]
        kidx = j * tk + lax.broadcasted_iota(I32, (tk, tq), 0)
        msk = (kt > t) | ((kt == t) & (kidx <= c))
        for hp in range(N_HEADS_A // 2):
            k = k_ref[pl.ds(j * tk, tk), hp * LANES:(hp + 1) * LANES]
            vt = vt_ref[hp, j]
            for h in (2 * hp, 2 * hp + 1):
                s = jnp.where(msk, _dot_nt(k, qm_scr[h]), NEG)
                _softmax_step(h, s, vt, m_scr, l_scr, acc_scr)
        return 0

    lax.fori_loop(0, n_tiles, attn_tile, 0)
    for hp in range(N_HEADS_A // 2):
        o_ref[:, hp * LANES:(hp + 1) * LANES] = _pair_output(hp, l_scr, acc_scr)


def _dsa_prompt(qa, qi, wt, ki_cat, k16, vt16, batch, seq, tq, tk):
    nq = seq // tq
    nk = seq // tk
    npair = N_HEADS_A // 2
    topk = min(DSA_TOPK, seq // 4)
    key_bits = max(1, (seq - 1).bit_length())
    once = pl.Buffered(1)
    qrow = lambda w: pl.BlockSpec((tq, w), lambda b, i: (b * nq + i, 0))
    return pl.pallas_call(
        functools.partial(_dsa_prompt_kernel, tq=tq, tk=tk, topk=topk, key_bits=key_bits),
        grid=(batch, nq),
        in_specs=[
            qrow(HA), qrow(HA),
            pl.BlockSpec((None, IDX_HEADS, tq), lambda b, i: (b, 0, i)),
            pl.BlockSpec((seq, 4 * IDX_DIM), lambda b, i: (b, 0), pipeline_mode=once),
            pl.BlockSpec((seq, HA), lambda b, i: (b, 0), pipeline_mode=once),
            pl.BlockSpec((None, npair, nk, LANES, tk), lambda b, i: (b, 0, 0, 0, 0), pipeline_mode=once),
        ],
        out_specs=qrow(HA),
        out_shape=jax.ShapeDtypeStruct((batch * seq, HA), BF16),
        scratch_shapes=[
            pltpu.VMEM((nk, tk, tq), I32),
            pltpu.VMEM((IDX_HEADS, tq, 4 * IDX_DIM), BF16),
            pltpu.VMEM((N_HEADS_A, tq, LANES), BF16),
            pltpu.VMEM((N_HEADS_A, 1, tq), F32),
            pltpu.VMEM((N_HEADS_A, 1, tq), F32),
            pltpu.VMEM((N_HEADS_A, LANES, tq), F32),
        ],
        compiler_params=_cparams("parallel", "parallel"),
        name="dsa_prompt",
    )(qa, qi, wt, ki_cat, k16, vt16)


def _block_mean_kernel(k_ref, o_ref):
    o_ref[...] = jnp.mean(k_ref[...], axis=0, keepdims=True)[None]


def _block_means(k2d):
    n = k2d.shape[0]
    nb = n // MOBA_BLOCK
    return pl.pallas_call(
        _block_mean_kernel,
        grid=(nb,),
        in_specs=[pl.BlockSpec((MOBA_BLOCK, HB), lambda i: (i, 0))],
        out_specs=pl.BlockSpec((1, 1, HB), lambda i: (i, 0, 0)),
        out_shape=jax.ShapeDtypeStruct((nb, 1, HB), F32),
        compiler_params=_cparams("parallel"),
        name="moba_block_means",
    )(k2d)


def _moba_prompt_kernel(q_ref, km_ref, k_ref, vt_ref, vt2_ref, o_ref,
                        sel_scr, qm_scr, m_scr, l_scr, acc_scr, *, t):
    qt = pl.program_id(2)
    q = q_ref[...]
    nb = km_ref.shape[0]
    lo_head = _pair_masks(q.shape)
    km_hi, km_lo = _split2(km_ref[...])
    blk = lax.broadcasted_iota(I32, (nb, t), 0)
    for hh in range(2):
        qh = jnp.where(lo_head if hh == 0 else ~lo_head, q, 0.0)
        q_hi, q_lo = _split2(qh)
        g = _dot3(km_hi, km_lo, q_hi, q_lo, _dot_nt)
        g = jnp.where(blk < qt, g, -jnp.inf)
        sel = jnp.zeros((nb, t), F32)
        for _ in range(MOBA_TOPK):
            mx = jnp.max(g, axis=0, keepdims=True)
            first = jnp.min(jnp.where(g == mx, blk, nb), axis=0, keepdims=True)
            pick = (blk == first) & (mx > -jnp.inf)
            sel = jnp.where(pick, 1.0, sel)
            g = jnp.where(blk == first, -jnp.inf, g)
        sel_scr[hh] = sel
    _store_pair_queries(q, qm_scr, 0)
    _softmax_init(m_scr, l_scr, acc_scr)

    def past_tile(j, _):
        k = k_ref[pl.ds(j * t, t), :]
        vt = vt_ref[j]
        for hh in range(2):
            chosen = sel_scr[hh, pl.ds(j, 1), :] > 0.0
            s = jnp.where(chosen, _dot_nt(k, qm_scr[hh]), NEG)
            _softmax_step(hh, s, vt, m_scr, l_scr, acc_scr)
        return 0

    def past_pair(jj, _):
        j = 2 * jj
        k = k_ref[pl.ds(j * t, 2 * t), :]
        vt = vt2_ref[jj]
        first = lax.broadcasted_iota(I32, (2 * t, t), 0) < t
        for hh in range(2):
            chosen = jnp.where(first, sel_scr[hh, pl.ds(j, 1), :], sel_scr[hh, pl.ds(j + 1, 1), :]) > 0.0
            s = jnp.where(chosen, _dot_nt(k, qm_scr[hh]), NEG)
            _softmax_step(hh, s, vt, m_scr, l_scr, acc_scr)
        return 0

    lax.fori_loop(0, qt // 2, past_pair, 0)

    @pl.when(qt % 2 == 1)
    def _():
        past_tile(qt - 1, 0)

    k = k_ref[pl.ds(qt * t, t), :]
    vt = vt_ref[qt]
    kidx, qidx = _key_query_index(t, t, 0, 0)
    for hh in range(2):
        s = jnp.where(kidx <= qidx, _dot_nt(k, qm_scr[hh]), NEG)
        _softmax_step(hh, s, vt, m_scr, l_scr, acc_scr)
    o_ref[...] = _pair_output(0, l_scr, acc_scr)


def _moba_prompt(q, kmean, k16, vt16, vt16_pairs, batch, seq):
    t = MOBA_T
    nq = seq // t
    nb = seq // MOBA_BLOCK
    npair = N_HEADS_B // 2
    return pl.pallas_call(
        functools.partial(_moba_prompt_kernel, t=t),
        grid=(batch, npair, nq),
        in_specs=[
            pl.BlockSpec((t, LANES), lambda b, p, i: (b * nq + i, p)),
            pl.BlockSpec((None, nb, LANES), lambda b, p, i: (b, 0, p)),
            pl.BlockSpec((seq, LANES), lambda b, p, i: (b, p)),
            pl.BlockSpec((None, None, nq, LANES, t), lambda b, p, i: (b, p, 0, 0, 0)),
            pl.BlockSpec((None, None, nq // 2, LANES, 2 * t), lambda b, p, i: (b, p, 0, 0, 0)),
        ],
        out_specs=pl.BlockSpec((t, LANES), lambda b, p, i: (b * nq + i, p)),
        out_shape=jax.ShapeDtypeStruct((batch * seq, HB), BF16),
        scratch_shapes=[
            pltpu.VMEM((2, nb, t), F32),
            pltpu.VMEM((2, t, LANES), BF16),
            pltpu.VMEM((2, 1, t), F32),
            pltpu.VMEM((2, 1, t), F32),
            pltpu.VMEM((2, LANES, t), F32),
        ],
        compiler_params=_cparams("parallel", "parallel", "parallel"),
        name="moba_prompt",
    )(q, kmean, k16, vt16, vt16_pairs)


def _fox_prompt_kernel(q_ref, cq_ref, ck_ref, k_ref, vt_ref, o_ref,
                       qm_scr, m_scr, l_scr, acc_scr, *, tq, tk):
    qt = pl.program_id(2)
    q0 = qt * tq
    _store_pair_queries(q_ref[...], qm_scr, 0)
    _softmax_init(m_scr, l_scr, acc_scr)
    cq = cq_ref[...]

    def tile(j, diagonal):
        k = k_ref[pl.ds(j * tk, tk), :]
        vt = vt_ref[j]
        ck = ck_ref[pl.ds(j * tk, tk), :]
        if diagonal:
            kidx, qidx = _key_query_index(tk, tq, j * tk, q0)
        logits = [_dot_nt(k, qm_scr[hh]) for hh in range(2)]
        for hh in range(2):
            s = logits[hh] + (cq[hh:hh + 1, :] - ck[:, hh:hh + 1])
            if diagonal:
                s = jnp.where(kidx <= qidx, s, NEG)
            _softmax_step(hh, s, vt, m_scr, l_scr, acc_scr)

    n_full = q0 // tk
    lax.fori_loop(0, n_full, lambda j, _: (tile(j, False), 0)[1], 0)
    tile(n_full, True)
    o_ref[...] = _pair_output(0, l_scr, acc_scr)


def _fox_prompt(q, cum_q, cum_k, k16, vt16, batch, seq, tq, tk):
    nq = seq // tq
    nk = seq // tk
    npair = N_HEADS_C // 2
    return pl.pallas_call(
        functools.partial(_fox_prompt_kernel, tq=tq, tk=tk),
        grid=(batch, npair, nq),
        in_specs=[
            pl.BlockSpec((tq, LANES), lambda b, p, i: (b * nq + i, p)),
            pl.BlockSpec((None, None, 2, tq), lambda b, p, i: (b, p, 0, i)),
            pl.BlockSpec((None, None, seq, 2), lambda b, p, i: (b, p, 0, 0)),
            pl.BlockSpec((seq, LANES), lambda b, p, i: (b, p)),
            pl.BlockSpec((None, None, nk, LANES, tk), lambda b, p, i: (b, p, 0, 0, 0)),
        ],
        out_specs=pl.BlockSpec((tq, LANES), lambda b, p, i: (b * nq + i, p)),
        out_shape=jax.ShapeDtypeStruct((batch * seq, HC), BF16),
        scratch_shapes=[
            pltpu.VMEM((2, tq, LANES), BF16),
            pltpu.VMEM((2, 1, tq), F32),
            pltpu.VMEM((2, 1, tq), F32),
            pltpu.VMEM((2, LANES, tq), F32),
        ],
        compiler_params=_cparams("parallel", "parallel", "parallel"),
        name="fox_prompt",
    )(q, cum_q, cum_k, k16, vt16)


def _router_kernel(x_ref, rwt_ref, rb_ref, g_ref):
    x_hi, x_lo = _split2(x_ref[...])
    rw_hi, rw_lo = _split2(rwt_ref[...])
    logits = _dot3(rw_hi, rw_lo, x_hi, x_lo, _dot_nt)
    scores = 1.0 / (1.0 + jnp.exp(-logits))
    biased = scores + rb_ref[...]
    rows = [biased[e:e + 1, :] for e in range(N_EXPERTS)]
    ninf = jnp.full_like(rows[0], -jnp.inf)

    def top2(vals):
        mx = functools.reduce(jnp.maximum, vals)
        picks1, found = [], jnp.zeros_like(mx) > 0
        for vv in vals:
            p = (vv == mx) & ~found
            found = found | p
            picks1.append(p)
        rest = [jnp.where(p, ninf, vv) for p, vv in zip(picks1, vals)]
        mx2 = functools.reduce(jnp.maximum, rest)
        picks2, found = [], jnp.zeros_like(mx) > 0
        for vv in rest:
            p = (vv == mx2) & ~found
            found = found | p
            picks2.append(p)
        return mx, mx2, picks1, picks2

    grp_score = []
    for gi in range(N_GROUPS):
        m1, m2, _, _ = top2(rows[gi * EXPERTS_PER_GROUP:(gi + 1) * EXPERTS_PER_GROUP])
        grp_score.append(m1 + m2)
    best = grp_score[0]
    g_sel = jnp.zeros_like(best, dtype=I32)
    for gi in range(1, N_GROUPS):
        better = grp_score[gi] > best
        best = jnp.where(better, grp_score[gi], best)
        g_sel = jnp.where(better, gi, g_sel)
    masked = [jnp.where(g_sel == e // EXPERTS_PER_GROUP, rows[e], ninf) for e in range(N_EXPERTS)]
    _, _, p1, p2 = top2(masked)
    zero = jnp.zeros_like(best)
    w1 = functools.reduce(jnp.add, [jnp.where(p1[e], scores[e:e + 1, :], zero) for e in range(N_EXPERTS)])
    w2 = functools.reduce(jnp.add, [jnp.where(p2[e], scores[e:e + 1, :], zero) for e in range(N_EXPERTS)])
    tot = w1 + w2
    for e in range(N_EXPERTS):
        g_ref[e:e + 1, :] = jnp.where(p1[e], w1 / tot, zero) + jnp.where(p2[e], w2 / tot, zero)


def _router(x2d, rwt, rb, tm):
    n = x2d.shape[0]
    full = lambda a: pl.BlockSpec(a.shape, lambda i: (0, 0))
    return pl.pallas_call(
        _router_kernel,
        grid=(n // tm,),
        in_specs=[pl.BlockSpec((tm, D_MODEL), lambda i: (i, 0)), full(rwt), full(rb)],
        out_specs=pl.BlockSpec((N_EXPERTS, tm), lambda i: (0, i)),
        out_shape=jax.ShapeDtypeStruct((N_EXPERTS, n), F32),
        compiler_params=_cparams("parallel"),
        name="moe_router",
    )(x2d, rwt, rb)


def _moe_kernel(x_ref, gates_ref, wg_ref, wu_ref, wd_ref, g_ref, b_ref, y_ref, xb_scr, acc_scr):
    e = pl.program_id(1)

    @pl.when(e == 0)
    def _():
        for i, part in enumerate(_lhs(x_ref[...], wg_ref)):
            xb_scr[i] = part
        acc_scr[...] = jnp.zeros_like(acc_scr)

    xs = tuple(xb_scr[i] for i in range(xb_scr.shape[0]))
    gate = _mm(xs, wg_ref[0])
    up = _mm(xs, wu_ref[0])
    h = gate * (1.0 / (1.0 + jnp.exp(-gate))) * up
    down = _mm(_lhs(h, wd_ref), wd_ref[0])
    gates = gates_ref[...]
    lane = lax.broadcasted_iota(I32, gates.shape, 1)
    w = jnp.sum(jnp.where(lane == e, gates, 0.0), axis=1, keepdims=True)
    acc_scr[...] += w * down

    @pl.when(e == pl.num_programs(1) - 1)
    def _():
        y_ref[...] = _layer_norm(ALPHA * x_ref[...] + acc_scr[...], g_ref[...], b_ref[...])


def _moe_ln(x2d, gates, wg16, wu16, wd16, g, b, tm):
    n = x2d.shape[0]
    full = lambda a: pl.BlockSpec(a.shape, lambda i, e: (0, 0))
    return pl.pallas_call(
        _moe_kernel,
        grid=(n // tm, N_EXPERTS),
        in_specs=[
            pl.BlockSpec((tm, D_MODEL), lambda i, e: (i, 0)),
            pl.BlockSpec((tm, N_EXPERTS), lambda i, e: (i, 0)),
            pl.BlockSpec((1, D_MODEL, D_EXPERT), lambda i, e: (e, 0, 0)),
            pl.BlockSpec((1, D_MODEL, D_EXPERT), lambda i, e: (e, 0, 0)),
            pl.BlockSpec((1, D_EXPERT, D_MODEL), lambda i, e: (e, 0, 0)),
            full(g), full(b),
        ],
        out_specs=pl.BlockSpec((tm, D_MODEL), lambda i, e: (i, 0)),
        out_shape=jax.ShapeDtypeStruct((n, D_MODEL), F32),
        scratch_shapes=[pltpu.VMEM((2 if wg16.dtype == F32 else 1, tm, D_MODEL), BF16),
                        pltpu.VMEM((tm, D_MODEL), F32)],
        compiler_params=_cparams("parallel", "arbitrary"),
        name="moe_experts_ln",
    )(x2d, gates, wg16, wu16, wd16, g, b)


def _page_view(cache):
    return jnp.moveaxis(cache, 1, -1)


def _one_key_page(col):
    shape = col.shape[:-1] + (PAGE_SIZE,)
    lane = lax.broadcasted_iota(I32, shape, len(shape) - 1)
    return jnp.where(lane == 0, col, 0.0)


def _new_key_bias(bias_new):
    lane = lax.broadcasted_iota(I32, (bias_new.shape[0], PAGE_SIZE), 1)
    return jnp.where(lane == 0, bias_new, NEG)


def _decode_logits(q_ref, k_refs, knew_ref, s_scr):
    n_heads = q_ref.shape[0]
    qb = jnp.broadcast_to(q_ref[...] * QK_SCALE, (n_heads, HEAD_DIM, PAGE_SIZE))
    for p, ref in enumerate(k_refs):
        s_scr[p] = jnp.sum(ref[...] * qb, axis=1)
    s_scr[len(k_refs)] = jnp.sum(_one_key_page(knew_ref[...]) * qb, axis=1)


def _decode_attend(s_scr, v_refs, vnew_ref, o_ref):
    s = s_scr[...]
    m = jnp.max(jnp.max(s, axis=0), axis=1, keepdims=True)
    s_scr[...] = jnp.exp(s - m)
    l = jnp.sum(jnp.sum(s_scr[...], axis=0), axis=1, keepdims=True)
    acc = _one_key_page(vnew_ref[...]) * s_scr[len(v_refs)][:, None, :]
    for p, ref in enumerate(v_refs):
        acc = acc + ref[...] * s_scr[p][:, None, :]
    o_ref[...] = jnp.sum(acc, axis=2, keepdims=True) / l[:, :, None]


def _dsa_sample_score_kernel(pt_ref, q_ref, w_ref, knew_ref, *rest, n_pages):
    page_refs = rest[:n_pages]
    s_ref = rest[n_pages]
    q = q_ref[...]
    q_hi, q_lo = _split2(q)
    w = w_ref[...]
    for p in range(n_pages):
        k_hi, k_lo = _split2(page_refs[p][...])
        d = _dot3(q_hi, q_lo, k_hi, k_lo, _dot)
        s_ref[p:p + 1, :] = jnp.sum(w * jnp.maximum(d, 0.0), axis=0, keepdims=True) * IDX_SCALE
    k_new = knew_ref[...][:, 0:IDX_DIM]
    d_new = jnp.sum(q * k_new, axis=1, keepdims=True)
    s_new = jnp.sum(w * jnp.maximum(d_new, 0.0), axis=0, keepdims=True) * IDX_SCALE
    lane = lax.broadcasted_iota(I32, (1, PAGE_SIZE), 1)
    s_ref[n_pages:n_pages + 1, :] = jnp.where(lane == 0, s_new, -jnp.inf)


def _dsa_sample_scores(page_table, q_i, w_i, tail, idx_view):
    nseq, n_pages = page_table.shape
    page = lambda p: pl.BlockSpec((None, IDX_DIM, PAGE_SIZE), lambda b, pt: (pt[b, p], 0, 0))
    grid_spec = pltpu.PrefetchScalarGridSpec(
        num_scalar_prefetch=1,
        grid=(nseq,),
        in_specs=[
            pl.BlockSpec((None, IDX_HEADS, IDX_DIM), lambda b, pt: (b, 0, 0)),
            pl.BlockSpec((None, IDX_HEADS, 1), lambda b, pt: (b, 0, 0)),
            pl.BlockSpec((None, 1, LANES), lambda b, pt: (b, 0, 0)),
        ] + [page(p) for p in range(n_pages)],
        out_specs=pl.BlockSpec((None, n_pages + 1, PAGE_SIZE), lambda b, pt: (b, 0, 0)),
    )
    return pl.pallas_call(
        functools.partial(_dsa_sample_score_kernel, n_pages=n_pages),
        grid_spec=grid_spec,
        out_shape=jax.ShapeDtypeStruct((nseq, n_pages + 1, PAGE_SIZE), F32),
        compiler_params=_cparams("parallel"),
        name="dsa_sample_scores",
    )(page_table, q_i, w_i, tail, *([idx_view] * n_pages))


def _select_kernel(s_ref, valid_ref, bias_ref, key_scr, *, topk, key_bits):
    n_tiles, tk, nseq = s_ref.shape
    for j in range(n_tiles):
        key_scr[j] = jnp.where(valid_ref[j] > 0.0, _sortable_key(s_ref[j]), INT_MIN)
    t, c = _topk_threshold(key_scr, n_tiles, topk, key_bits)
    for j in range(n_tiles):
        kt = key_scr[j]
        kidx = j * tk + lax.broadcasted_iota(I32, kt.shape, 0)
        bias_ref[j] = jnp.where((kt > t) | ((kt == t) & (kidx <= c)), 0.0, NEG)


def _select_topk_bias(scores_t, valid_t, topk):
    n_chunks = scores_t.shape[0]
    key_bits = max(1, (n_chunks * PAGE_SIZE - 1).bit_length())
    return pl.pallas_call(
        functools.partial(_select_kernel, topk=topk, key_bits=key_bits),
        out_shape=jax.ShapeDtypeStruct(scores_t.shape, F32),
        scratch_shapes=[pltpu.VMEM(scores_t.shape, I32)],
        compiler_params=pltpu.CompilerParams(vmem_limit_bytes=VMEM_LIMIT),
        name="dsa_sample_select",
    )(scores_t, valid_t)


def _decode_call(kernel_fn, name, page_table, n_heads, head_inputs, extra_inputs, extra_specs, k_view, v_view):
    nseq, n_pages = page_table.shape
    hd = pl.BlockSpec((None, n_heads, HEAD_DIM, 1), lambda b, pt: (b, 0, 0, 0))
    page = lambda p: pl.BlockSpec((None, n_heads, HEAD_DIM, PAGE_SIZE), lambda b, pt: (pt[b, p], 0, 0, 0))
    pages = [page(p) for p in range(n_pages)]
    grid_spec = pltpu.PrefetchScalarGridSpec(
        num_scalar_prefetch=1,
        grid=(nseq,),
        in_specs=[hd] * len(head_inputs) + list(extra_specs) + pages + pages,
        out_specs=hd,
        scratch_shapes=[pltpu.VMEM((n_pages + 1, n_heads, PAGE_SIZE), F32)],
    )
    return pl.pallas_call(
        functools.partial(kernel_fn, n_pages=n_pages),
        grid_spec=grid_spec,
        out_shape=jax.ShapeDtypeStruct((nseq, n_heads, HEAD_DIM, 1), F32),
        compiler_params=_cparams("parallel"),
        name=name,
    )(page_table, *head_inputs, *extra_inputs, *([k_view] * n_pages), *([v_view] * n_pages))


def _dsa_sample_attn_kernel(pt_ref, q_ref, knew_ref, vnew_ref, bias_ref, *rest, n_pages):
    k_refs, v_refs = rest[:n_pages], rest[n_pages:2 * n_pages]
    o_ref, s_scr = rest[2 * n_pages], rest[2 * n_pages + 1]
    _decode_logits(q_ref, k_refs, knew_ref, s_scr)
    s_scr[...] = s_scr[...] + bias_ref[...][:, None, :]
    _decode_attend(s_scr, v_refs, vnew_ref, o_ref)


def _dsa_sample_attn(page_table, q, k_new, v_new, bias, k_view, v_view):
    n_pages = page_table.shape[1]
    spec = pl.BlockSpec((None, n_pages + 1, PAGE_SIZE), lambda b, pt: (b, 0, 0))
    return _decode_call(_dsa_sample_attn_kernel, "dsa_sample_attn", page_table, N_HEADS_A,
                        (q, k_new, v_new), (bias,), (spec,), k_view, v_view)


def _moba_sample_kernel(pt_ref, q_ref, knew_ref, vnew_ref, *rest, n_pages):
    k_refs, v_refs = rest[:n_pages], rest[n_pages:2 * n_pages]
    o_ref, s_scr = rest[2 * n_pages], rest[2 * n_pages + 1]
    ppb = MOBA_BLOCK // PAGE_SIZE
    n_blocks = n_pages // ppb
    q = q_ref[...]
    gates = []
    for n in range(n_blocks):
        ksum = functools.reduce(jnp.add, [k_refs[n * ppb + i][...] for i in range(ppb)])
        k_mean = jnp.sum(ksum, axis=2, keepdims=True) / float(MOBA_BLOCK)
        gates.append(jnp.sum(q * k_mean, axis=1))
    chosen = [jnp.zeros_like(gates[0]) > 0 for _ in range(n_blocks)]
    for _ in range(min(MOBA_TOPK, n_blocks + 1)):
        mx = functools.reduce(jnp.maximum, gates)
        found = jnp.zeros_like(mx) > 0
        for n in range(n_blocks):
            pick = (gates[n] == mx) & ~found & (mx > -jnp.inf)
            found = found | pick
            chosen[n] = chosen[n] | pick
            gates[n] = jnp.where(pick, -jnp.inf, gates[n])
    _decode_logits(q_ref, k_refs, knew_ref, s_scr)
    for p in range(n_pages):
        s_scr[p] = s_scr[p] + jnp.where(chosen[p // ppb], 0.0, NEG)
    s_scr[n_pages] = s_scr[n_pages] + _new_key_bias(jnp.zeros((1, 1), F32))
    _decode_attend(s_scr, v_refs, vnew_ref, o_ref)


def _moba_sample(page_table, q, k_new, v_new, k_view, v_view):
    return _decode_call(_moba_sample_kernel, "moba_sample", page_table, N_HEADS_B,
                        (q, k_new, v_new), (), (), k_view, v_view)


def _fox_sample_kernel(pt_ref, q_ref, knew_ref, vnew_ref, lnew_ref, *rest, n_pages):
    l_refs = rest[:n_pages]
    k_refs, v_refs = rest[n_pages:2 * n_pages], rest[2 * n_pages:3 * n_pages]
    o_ref, s_scr = rest[3 * n_pages], rest[3 * n_pages + 1]
    r = lax.broadcasted_iota(I32, (PAGE_SIZE, PAGE_SIZE), 0)
    cc = lax.broadcasted_iota(I32, (PAGE_SIZE, PAGE_SIZE), 1)
    upper = jnp.where(r <= cc, 1.0, 0.0).astype(BF16)
    carry = jnp.zeros((N_HEADS_C, 1), F32)
    cums = []
    for p in range(n_pages):
        hi, mid, lo = _split3(l_refs[p][...])
        cum = (_dot(hi, upper) + _dot(mid, upper) + _dot(lo, upper)) + carry
        cums.append(cum)
        carry = cum[:, PAGE_SIZE - 1:PAGE_SIZE]
    cum_q = carry + lnew_ref[...]
    _decode_logits(q_ref, k_refs, knew_ref, s_scr)
    for p in range(n_pages):
        s_scr[p] = s_scr[p] + (cum_q - cums[p])
    s_scr[n_pages] = s_scr[n_pages] + _new_key_bias(cum_q - cum_q)
    _decode_attend(s_scr, v_refs, vnew_ref, o_ref)


def _fox_sample(page_table, q, k_new, v_new, logf_new, logf_view, k_view, v_view):
    n_pages = page_table.shape[1]
    lnew = pl.BlockSpec((None, N_HEADS_C, 1), lambda b, pt: (b, 0, 0))
    lpage = lambda p: pl.BlockSpec((None, N_HEADS_C, PAGE_SIZE), lambda b, pt: (pt[b, p], 0, 0))
    return _decode_call(_fox_sample_kernel, "fox_sample", page_table, N_HEADS_C,
                        (q, k_new, v_new), (logf_new,) + (logf_view,) * n_pages,
                        (lnew,) + tuple(lpage(p) for p in range(n_pages)), k_view, v_view)


def _pad_cols(w, width):
    return jnp.pad(w, ((0, 0), (0, width - w.shape[1])))


def _rope_inv_freq():
    half = HEAD_DIM // 8
    inv = ROPE_THETA ** (-jnp.arange(half, dtype=F32) / half)
    per_head = jnp.concatenate([inv, inv, jnp.zeros((HEAD_DIM - 2 * half,), F32)])
    return jnp.tile(per_head, LANES // HEAD_DIM)[None, :]


def _transposed_values(v16, batch, seq, tk):
    npair = v16.shape[1] // LANES
    return v16.reshape(batch, seq // tk, tk, npair, LANES).transpose(0, 3, 1, 4, 2)


def _ab_layer_prompt(x2d, w16, w_out16, invf, g, b, batch, seq):
    pos = jnp.tile(jnp.arange(seq, dtype=F32), batch)[:, None]
    qa, ka, va, qb, kb, vb, qi, tail, ka16, va16, kb16, vb16, ki_cat = _ab_project(x2d, w16, pos, invf, PROJ_TM)
    k_idx = tail[:, :IDX_DIM]
    wt = tail[:, IDX_DIM:IDX_DIM + IDX_HEADS].reshape(batch, seq, IDX_HEADS).transpose(0, 2, 1)
    o_a = _dsa_prompt(qa, qi, wt, ki_cat, ka16, _transposed_values(va16, batch, seq, DSA_TK),
                      batch, seq, DSA_TQ, DSA_TK)
    kmean = _block_means(kb).reshape(batch, seq // MOBA_BLOCK, HB)
    o_b = _moba_prompt(qb, kmean, kb16, _transposed_values(vb16, batch, seq, MOBA_T),
                       _transposed_values(vb16, batch, seq, 2 * MOBA_T), batch, seq)
    o = jnp.concatenate([o_a, o_b], axis=1)
    y = _out_proj_ln(o, w_out16, x2d, g, b, PROJ_TM)
    return y, (ka, va, k_idx, kb, vb)


def _ab_layer_sample(x2d, w16, w_out16, invf, g, b, past_len, page_table,
                     cache_a_k, cache_a_v, cache_a_idx, cache_b_k, cache_b_v):
    nseq = x2d.shape[0]
    n_pages = page_table.shape[1]
    pos = jnp.full((nseq, 1), past_len, F32)
    qa, ka, va, qb, kb, vb, qi, tail = _ab_project(x2d, w16, pos, invf, nseq)[:8]
    k_idx = tail[:, :IDX_DIM]
    w_i = tail[:, IDX_DIM:IDX_DIM + IDX_HEADS]
    col = lambda a: a.reshape(nseq, -1, HEAD_DIM, 1)
    scores = _dsa_sample_scores(page_table, qi.reshape(nseq, IDX_HEADS, IDX_DIM), w_i.reshape(nseq, IDX_HEADS, 1),
                                tail.reshape(nseq, 1, LANES), _page_view(cache_a_idx))
    n_keys = n_pages * PAGE_SIZE + 1
    valid = (jnp.arange((n_pages + 1) * PAGE_SIZE) < n_keys).astype(F32)
    valid_t = jnp.broadcast_to(valid.reshape(n_pages + 1, PAGE_SIZE, 1), (n_pages + 1, PAGE_SIZE, nseq))
    bias_t = _select_topk_bias(scores.transpose(1, 2, 0), valid_t, min(DSA_TOPK, n_keys // 4))
    o_a = _dsa_sample_attn(page_table, col(qa), col(ka), col(va), bias_t.transpose(2, 0, 1),
                           _page_view(cache_a_k), _page_view(cache_a_v))
    o_b = _moba_sample(page_table, col(qb), col(kb), col(vb), _page_view(cache_b_k), _page_view(cache_b_v))
    o = jnp.concatenate([o_a.reshape(nseq, HA), o_b.reshape(nseq, HB)], axis=1)
    y = _out_proj_ln(o, w_out16, x2d, g, b, nseq)
    return y, (ka, va, k_idx, kb, vb)


def _fox_layer_prompt(x2d, w16, bf_pad, w_out16, g, b, batch, seq):
    q, k, v, logf, cum, k16, v16 = _fox_project(x2d, w16, bf_pad, PROJ_TM, seq)
    npair = N_HEADS_C // 2
    cum_k = cum[:, :N_HEADS_C].reshape(batch, seq, npair, 2).transpose(0, 2, 1, 3)
    cum_q = cum_k.transpose(0, 1, 3, 2)
    o = _fox_prompt(q, cum_q, cum_k, k16, _transposed_values(v16, batch, seq, FOX_TK),
                    batch, seq, FOX_TQ, FOX_TK)
    y = _out_proj_ln(o, w_out16, x2d, g, b, PROJ_TM)
    return y, (k, v, logf[:, :N_HEADS_C])


def _fox_layer_sample(x2d, w16, bf_pad, w_out16, g, b, page_table, cache_c_k, cache_c_v, cache_c_logf):
    nseq = x2d.shape[0]
    q, k, v, logf, _, _, _ = _fox_project(x2d, w16, bf_pad, nseq, nseq)
    col = lambda a: a.reshape(nseq, N_HEADS_C, HEAD_DIM, 1)
    o = _fox_sample(page_table, col(q), col(k), col(v), logf[:, :N_HEADS_C].reshape(nseq, N_HEADS_C, 1),
                    _page_view(cache_c_logf), _page_view(cache_c_k), _page_view(cache_c_v))
    y = _out_proj_ln(o.reshape(nseq, HC), w_out16, x2d, g, b, nseq)
    return y, (k, v, logf[:, :N_HEADS_C])


def _moe_layer(x2d, rwt, rb, wg16, wu16, wd16, g, b, tm):
    gates_t = _router(x2d, rwt, rb, tm)
    return _moe_ln(x2d, gates_t.T, wg16, wu16, wd16, g, b, tm)


def kernel(x_prompt, x_sample, cache_a_k, cache_a_v, cache_a_idx, cache_b_k, cache_b_v, cache_c_k, cache_c_v, cache_c_logf, page_table, w_in_ab, w_out_ab, w_in_fox, b_forget, w_out_fox, ln_mix_g, ln_mix_b, ln_ffn_g, ln_ffn_b, router_w, router_bias, exp_w_gate, exp_w_up, exp_w_down):
    batch, seq, _ = x_prompt.shape
    nseq = x_sample.shape[0]
    past_len = page_table.shape[1] * PAGE_SIZE

    w_ab32 = _pad_cols(w_in_ab, 7 * HA + LANES)
    w_fox32 = _pad_cols(w_in_fox, 3 * HC + LANES)
    w_ab16, w_out_ab16 = w_ab32.astype(BF16), w_out_ab.astype(BF16)
    w_fox16, w_out_fox16 = w_fox32.astype(BF16), w_out_fox.astype(BF16)
    bf_pad = jnp.pad(b_forget, (0, LANES - N_HEADS_C))[None, :]
    invf = _rope_inv_freq()
    rwt = router_w.T
    rb = router_bias[:, None]
    wg16, wu16, wd16 = exp_w_gate.astype(BF16), exp_w_up.astype(BF16), exp_w_down.astype(BF16)
    row = lambda a, i: a[i][None, :]

    xp = x_prompt.reshape(batch * seq, D_MODEL)
    xs = x_sample.reshape(nseq, D_MODEL)

    xp, (pa_k, pa_v, pa_idx, pb_k, pb_v) = _ab_layer_prompt(
        xp, w_ab16, w_out_ab16, invf, row(ln_mix_g, 0), row(ln_mix_b, 0), batch, seq)
    xs, (sa_k, sa_v, sa_idx, sb_k, sb_v) = _ab_layer_sample(
        xs, w_ab32, w_out_ab, invf, row(ln_mix_g, 0), row(ln_mix_b, 0), past_len, page_table,
        cache_a_k, cache_a_v, cache_a_idx, cache_b_k, cache_b_v)
    xp = _moe_layer(xp, rwt, rb, wg16[0], wu16[0], wd16[0], row(ln_ffn_g, 0), row(ln_ffn_b, 0), MOE_TM)
    xs = _moe_layer(xs, rwt, rb, exp_w_gate[0], exp_w_up[0], exp_w_down[0],
                    row(ln_ffn_g, 0), row(ln_ffn_b, 0), nseq)

    xp, (pc_k, pc_v, pc_logf) = _fox_layer_prompt(
        xp, w_fox16, bf_pad, w_out_fox16, row(ln_mix_g, 1), row(ln_mix_b, 1), batch, seq)
    xs, (sc_k, sc_v, sc_logf) = _fox_layer_sample(
        xs, w_fox32, bf_pad, w_out_fox, row(ln_mix_g, 1), row(ln_mix_b, 1), page_table,
        cache_c_k, cache_c_v, cache_c_logf)
    xp = _moe_layer(xp, rwt, rb, wg16[1], wu16[1], wd16[1], row(ln_ffn_g, 1), row(ln_ffn_b, 1), MOE_TM)
    xs = _moe_layer(xs, rwt, rb, exp_w_gate[1], exp_w_up[1], exp_w_down[1],
                    row(ln_ffn_g, 1), row(ln_ffn_b, 1), nseq)

    hd = lambda a, nh, lead: a.reshape(*lead, nh, HEAD_DIM)
    lp, ls = (batch, seq), (nseq, 1)
    return (xp.reshape(batch, seq, D_MODEL), xs.reshape(nseq, 1, D_MODEL),
            hd(pa_k, N_HEADS_A, lp), hd(pa_v, N_HEADS_A, lp), pa_idx.reshape(batch, seq, IDX_DIM),
            hd(pb_k, N_HEADS_B, lp), hd(pb_v, N_HEADS_B, lp),
            hd(pc_k, N_HEADS_C, lp), hd(pc_v, N_HEADS_C, lp), pc_logf.reshape(batch, seq, N_HEADS_C),
            hd(sa_k, N_HEADS_A, ls), hd(sa_v, N_HEADS_A, ls), sa_idx.reshape(nseq, 1, IDX_DIM),
            hd(sb_k, N_HEADS_B, ls), hd(sb_v, N_HEADS_B, ls),
            hd(sc_k, N_HEADS_C, ls), hd(sc_v, N_HEADS_C, ls), sc_logf.reshape(nseq, 1, N_HEADS_C))
```

```python
import functools

import jax
import jax.numpy as jnp
from jax import lax
from jax.experimental import pallas as pl
from jax.experimental.pallas import tpu as pltpu

F32 = jnp.float32
BF16 = jnp.bfloat16
I32 = jnp.int32

D_MODEL = 1024
DEPTH = 2
PAGE_SIZE = 128
HEAD_DIM = 64
N_HEADS_A = 8
N_HEADS_B = 8
N_HEADS_C = 16
IDX_HEADS = 8
IDX_DIM = 64
DSA_TOPK = 256
MOBA_BLOCK = 256
MOBA_TOPK = 3
ROPE_THETA = 500000.0
N_EXPERTS = 16
N_GROUPS = 4
EXPERTS_PER_GROUP = N_EXPERTS // N_GROUPS
D_EXPERT = 512
ALPHA = (2 * DEPTH) ** 0.25
LN_EPS = 1e-5
HA = N_HEADS_A * HEAD_DIM
HB = N_HEADS_B * HEAD_DIM
HC = N_HEADS_C * HEAD_DIM
QK_SCALE = HEAD_DIM ** -0.5
IDX_SCALE = IDX_DIM ** -0.5

LANES = 128
NEG = -1e30
INT_MIN = -2 ** 31
VMEM_LIMIT = 56 * 2 ** 20

PROJ_TM = 256
DSA_TQ = 256
DSA_TK = 512
MOBA_T = MOBA_BLOCK
FOX_TQ = 512
FOX_TK = 1024
MOE_TM = 1024


def _cparams(*sem):
    return pltpu.CompilerParams(dimension_semantics=sem, vmem_limit_bytes=VMEM_LIMIT)


def _dot(a, b):
    return jnp.dot(a, b, preferred_element_type=F32)


def _dot_nt(a, b):
    return lax.dot_general(a, b, (((1,), (1,)), ((), ())), preferred_element_type=F32)


def _split2(x):
    hi = x.astype(BF16)
    lo = (x - hi.astype(F32)).astype(BF16)
    return hi, lo


def _split3(x):
    hi = x.astype(BF16)
    r = x - hi.astype(F32)
    mid = r.astype(BF16)
    lo = (r - mid.astype(F32)).astype(BF16)
    return hi, mid, lo


def _dot3(a_hi, a_lo, b_hi, b_lo, dot):
    return dot(a_hi, b_hi) + (dot(a_hi, b_lo) + dot(a_lo, b_hi))


def _lhs(x, w_ref):
    return _split2(x) if w_ref.dtype == F32 else (x.astype(BF16),)


def _mm(xs, w):
    if len(xs) == 1:
        return _dot(xs[0], w)
    w_hi, w_lo = _split2(w)
    return _dot3(xs[0], xs[1], w_hi, w_lo, _dot)


def _layer_norm(z, g, b):
    mu = jnp.mean(z, axis=-1, keepdims=True)
    d = z - mu
    var = jnp.mean(d * d, axis=-1, keepdims=True)
    return d * lax.rsqrt(var + LN_EPS) * g + b


def _log_sigmoid(z):
    return -(jnp.maximum(-z, 0.0) + jnp.log1p(jnp.exp(-jnp.abs(z))))


def _rotary_tables(pos_ref, invf_ref):
    ang = pos_ref[...] * invf_ref[...]
    c = jnp.cos(ang)
    s = jnp.sin(ang)
    f = lax.broadcasted_iota(I32, ang.shape, 1) % HEAD_DIM
    s_up = jnp.where(f < 8, -s, 0.0)
    s_dn = jnp.where(f >= 8, s, 0.0)
    return c, s_up, s_dn


def _rotate(h, c, s_up, s_dn):
    outs = []
    for j in range(h.shape[1] // LANES):
        hc = h[:, j * LANES:(j + 1) * LANES]
        outs.append(hc * c + pltpu.roll(hc, LANES - 8, 1) * s_up + pltpu.roll(hc, 8, 1) * s_dn)
    return outs[0] if len(outs) == 1 else jnp.concatenate(outs, axis=1)


def _ab_proj_kernel(x_ref, w_ref, pos_ref, invf_ref,
                    qa_ref, ka_ref, va_ref, qb_ref, kb_ref, vb_ref, qi_ref, tail_ref,
                    ka16_ref, va16_ref, kb16_ref, vb16_ref, kic_ref):
    xs = _lhs(x_ref[...], w_ref)
    c, s_up, s_dn = _rotary_tables(pos_ref, invf_ref)

    def seg(j, width=HA):
        return _mm(xs, w_ref[:, j * HA:j * HA + width])

    qa_ref[...] = _rotate(seg(0), c, s_up, s_dn)
    ka = _rotate(seg(1), c, s_up, s_dn)
    ka_ref[...] = ka
    ka16_ref[...] = ka.astype(BF16)
    va = seg(2)
    va_ref[...] = va
    va16_ref[...] = va.astype(BF16)
    qb_ref[...] = _rotate(seg(3), c, s_up, s_dn)
    kb = _rotate(seg(4), c, s_up, s_dn)
    kb_ref[...] = kb
    kb16_ref[...] = kb.astype(BF16)
    vb = seg(5)
    vb_ref[...] = vb
    vb16_ref[...] = vb.astype(BF16)
    qi_ref[...] = _rotate(seg(6), c, s_up, s_dn)
    t = seg(7, LANES)
    lane = lax.broadcasted_iota(I32, t.shape, 1)
    is_key = lane < IDX_DIM
    ct = jnp.where(is_key, c, IDX_HEADS ** -0.5)
    tail = (t * ct + pltpu.roll(t, LANES - 8, 1) * jnp.where(is_key, s_up, 0.0)
            + pltpu.roll(t, 8, 1) * jnp.where(is_key, s_dn, 0.0))
    tail_ref[...] = tail
    ki_hi, ki_lo = _split2(tail[:, 0:IDX_DIM])
    kic_ref[...] = jnp.concatenate([ki_hi, ki_lo, ki_hi, ki_lo], axis=1)


def _ab_project(x2d, w16, pos, invf, tm):
    n = x2d.shape[0]
    wide = jax.ShapeDtypeStruct((n, HA), F32)
    wide16 = jax.ShapeDtypeStruct((n, HA), BF16)
    row = lambda w: pl.BlockSpec((tm, w), lambda i: (i, 0))
    full = lambda a: pl.BlockSpec(a.shape, lambda i: (0, 0))
    return pl.pallas_call(
        _ab_proj_kernel,
        grid=(n // tm,),
        in_specs=[row(D_MODEL), full(w16), row(1), full(invf)],
        out_specs=[row(HA)] * 7 + [row(LANES)] + [row(HA)] * 4 + [row(4 * IDX_DIM)],
        out_shape=([wide] * 7 + [jax.ShapeDtypeStruct((n, LANES), F32)] + [wide16] * 4
                   + [jax.ShapeDtypeStruct((n, 4 * IDX_DIM), BF16)]),
        compiler_params=_cparams("parallel"),
        name="ab_project",
    )(x2d, w16, pos, invf)


def _fox_proj_kernel(x_ref, w_ref, bf_ref, q_ref, k_ref, v_ref, logf_ref, cum_ref,
                     k16_ref, v16_ref, carry_ref, *, tiles_per_seq):
    i = pl.program_id(0)
    xs = _lhs(x_ref[...], w_ref)
    q_ref[...] = _mm(xs, w_ref[:, 0:HC])
    k = _mm(xs, w_ref[:, HC:2 * HC])
    k_ref[...] = k
    k16_ref[...] = k.astype(BF16)
    v = _mm(xs, w_ref[:, 2 * HC:3 * HC])
    v_ref[...] = v
    v16_ref[...] = v.astype(BF16)
    f = _mm(xs, w_ref[:, 3 * HC:3 * HC + LANES])
    logf = _log_sigmoid(f + bf_ref[...])
    logf_ref[...] = logf

    @pl.when(i % tiles_per_seq == 0)
    def _():
        carry_ref[...] = jnp.zeros_like(carry_ref)

    tm = logf.shape[0]
    r = lax.broadcasted_iota(I32, (tm, tm), 0)
    cc = lax.broadcasted_iota(I32, (tm, tm), 1)
    tril = jnp.where(cc <= r, 1.0, 0.0).astype(BF16)
    hi, mid, lo = _split3(logf)
    cum = (_dot(tril, hi) + _dot(tril, mid) + _dot(tril, lo)) + carry_ref[...]
    cum_ref[...] = cum
    carry_ref[...] = cum[tm - 1:tm, :]


def _fox_project(x2d, w16, bf_pad, tm, rows_per_seq):
    n = x2d.shape[0]
    wide = jax.ShapeDtypeStruct((n, HC), F32)
    wide16 = jax.ShapeDtypeStruct((n, HC), BF16)
    small = jax.ShapeDtypeStruct((n, LANES), F32)
    row = lambda w: pl.BlockSpec((tm, w), lambda i: (i, 0))
    full = lambda a: pl.BlockSpec(a.shape, lambda i: (0, 0))
    return pl.pallas_call(
        functools.partial(_fox_proj_kernel, tiles_per_seq=rows_per_seq // tm),
        grid=(n // tm,),
        in_specs=[row(D_MODEL), full(w16), full(bf_pad)],
        out_specs=[row(HC)] * 3 + [row(LANES)] * 2 + [row(HC)] * 2,
        out_shape=[wide] * 3 + [small] * 2 + [wide16] * 2,
        scratch_shapes=[pltpu.VMEM((1, LANES), F32)],
        compiler_params=_cparams("arbitrary"),
        name="fox_project",
    )(x2d, w16, bf_pad)


def _out_ln_kernel(o_ref, w_ref, x_ref, g_ref, b_ref, y_ref):
    o = o_ref[...]
    m = _mm(_lhs(o, w_ref) if o.dtype == F32 else (o,), w_ref[...])
    y_ref[...] = _layer_norm(ALPHA * x_ref[...] + m, g_ref[...], b_ref[...])


def _out_proj_ln(o16, w16, x2d, g, b, tm):
    n, k = o16.shape
    row = lambda w: pl.BlockSpec((tm, w), lambda i: (i, 0))
    full = lambda a: pl.BlockSpec(a.shape, lambda i: (0, 0))
    return pl.pallas_call(
        _out_ln_kernel,
        grid=(n // tm,),
        in_specs=[row(k), full(w16), row(D_MODEL), full(g), full(b)],
        out_specs=row(D_MODEL),
        out_shape=jax.ShapeDtypeStruct((n, D_MODEL), F32),
        compiler_params=_cparams("parallel"),
        name="out_proj_ln",
    )(o16, w16, x2d, g, b)


def _softmax_init(m_scr, l_scr, acc_scr):
    m_scr[...] = jnp.full(m_scr.shape, NEG, F32)
    l_scr[...] = jnp.zeros(l_scr.shape, F32)
    acc_scr[...] = jnp.zeros(acc_scr.shape, F32)


def _softmax_step(h, s, vt, m_scr, l_scr, acc_scr):
    m_old = m_scr[h]
    m_new = jnp.maximum(m_old, jnp.max(s, axis=0, keepdims=True))
    alpha = jnp.exp(m_old - m_new)
    p = jnp.exp(s - m_new)
    l_scr[h] = alpha * l_scr[h] + jnp.sum(p, axis=0, keepdims=True)
    acc_scr[h] = alpha * acc_scr[h] + _dot(vt, p.astype(BF16))
    m_scr[h] = m_new


def _pair_masks(shape):
    lane = lax.broadcasted_iota(I32, shape, 1)
    return lane < HEAD_DIM


def _store_pair_queries(q, qm_scr, base):
    lo_head = _pair_masks(q.shape)
    qs = (q * QK_SCALE).astype(BF16)
    zero = jnp.zeros_like(qs)
    qm_scr[base] = jnp.where(lo_head, qs, zero)
    qm_scr[base + 1] = jnp.where(lo_head, zero, qs)


def _pair_output(hp, l_scr, acc_scr):
    a0 = acc_scr[2 * hp] / l_scr[2 * hp]
    a1 = acc_scr[2 * hp + 1] / l_scr[2 * hp + 1]
    first = lax.broadcasted_iota(I32, a0.shape, 0) < HEAD_DIM
    return jnp.where(first, a0, a1).T.astype(BF16)


def _key_query_index(tk, tq, k0, q0):
    kidx = k0 + lax.broadcasted_iota(I32, (tk, tq), 0)
    qidx = q0 + lax.broadcasted_iota(I32, (tk, tq), 1)
    return kidx, qidx


def _sortable_key(x):
    b = pltpu.bitcast(x, I32)
    return jnp.where(b < 0, INT_MIN - b, b)


def _count_keys(key_scr, n_tiles, pred):
    _, tk, tq = key_scr.shape

    def body(j, acc):
        for cidx in range(tk // 8):
            chunk = key_scr[j, cidx * 8:(cidx + 1) * 8, :]
            acc = acc + jnp.where(pred(chunk, j * tk + cidx * 8), 1, 0)
        return acc

    acc = lax.fori_loop(0, n_tiles, body, jnp.zeros((8, tq), I32))
    return jnp.sum(acc, axis=0, keepdims=True)


def _topk_threshold(key_scr, n_tiles, topk, key_bits):
    _, _, tq = key_scr.shape
    rows8 = lambda x: jnp.broadcast_to(x, (8, tq))

    def bit_body(i, t):
        cand = rows8(t + jnp.left_shift(jnp.int32(1), 31 - i))
        cnt = _count_keys(key_scr, n_tiles, lambda k, k0: k >= cand)
        return jnp.where(cnt >= topk, cand[0:1, :], t)

    t = lax.fori_loop(0, 32, bit_body, jnp.full((1, tq), INT_MIN, I32))
    t8 = rows8(t)
    n_gt = _count_keys(key_scr, n_tiles, lambda k, k0: k > t8)
    n_eq = _count_keys(key_scr, n_tiles, lambda k, k0: k == t8)
    need = topk - n_gt
    excess = jnp.max(jnp.where(n_eq > need, 1, 0)) > 0

    def tie_search():
        sub = lax.broadcasted_iota(I32, (8, tq), 0)

        def body(i, x):
            cand = rows8(x + jnp.left_shift(jnp.int32(1), key_bits - 1 - i))
            cnt = _count_keys(key_scr, n_tiles, lambda k, k0: (k == t8) & (sub + k0 < cand))
            return jnp.where(cnt < need, cand[0:1, :], x)

        return lax.fori_loop(0, key_bits, body, jnp.zeros((1, tq), I32))

    c = lax.cond(excess, tie_search, lambda: jnp.full((1, tq), 2 ** 31 - 1, I32))
    c = jnp.where(t == INT_MIN, -1, c)
    return t, c


def _dsa_prompt_kernel(qa_ref, qi_ref, wt_ref, kic_ref, k_ref, vt_ref, o_ref,
                       key_scr, qic_scr, qm_scr, m_scr, l_scr, acc_scr,
                       *, tq, tk, topk, key_bits):
    qt = pl.program_id(1)
    q0 = qt * tq
    n_tiles = (q0 + tq + tk - 1) // tk

    qi = qi_ref[...]
    for h in range(IDX_HEADS):
        hi, lo = _split2(qi[:, h * IDX_DIM:(h + 1) * IDX_DIM])
        qic_scr[h] = jnp.concatenate([hi, hi, lo, lo], axis=1)
    wt = wt_ref[...]

    def score_tile(j, diagonal):
        kc = kic_ref[pl.ds(j * tk, tk), :]
        sc = jnp.zeros((tk, tq), F32)
        for h in range(IDX_HEADS):
            d = _dot_nt(kc, qic_scr[h])
            sc = sc + wt[h:h + 1, :] * jnp.maximum(d, 0.0)
        key = _sortable_key(sc * IDX_SCALE)
        if diagonal:
            kidx, qidx = _key_query_index(tk, tq, j * tk, q0)
            key = jnp.where(kidx <= qidx, key, INT_MIN)
        key_scr[j] = key

    lax.fori_loop(0, n_tiles - 1, lambda j, _: (score_tile(j, False), 0)[1], 0)
    score_tile(n_tiles - 1, True)

    t, c = _topk_threshold(key_scr, n_tiles, topk, key_bits)

    qa = qa_ref[...]
    for hp in range(N_HEADS_A // 2):
        _store_pair_queries(qa[:, hp * LANES:(hp + 1) * LANES], qm_scr, 2 * hp)
    _softmax_init(m_scr, l_scr, acc_scr)

    def attn_tile(j, _):
        kt = key_scr[j]
        kidx = j * tk + lax.broadcasted_iota(I32, (tk, tq), 0)
        msk = (kt > t) | ((kt == t) & (kidx <= c))
        for hp in range(N_HEADS_A // 2):
            k = k_ref[pl.ds(j * tk, tk), hp * LANES:(hp + 1) * LANES]
            vt = vt_ref[hp, j]
            for h in (2 * hp, 2 * hp + 1):
                s = jnp.where(msk, _dot_nt(k, qm_scr[h]), NEG)
                _softmax_step(h, s, vt, m_scr, l_scr, acc_scr)
        return 0

    lax.fori_loop(0, n_tiles, attn_tile, 0)
    for hp in range(N_HEADS_A // 2):
        o_ref[:, hp * LANES:(hp + 1) * LANES] = _pair_output(hp, l_scr, acc_scr)


def _dsa_prompt(qa, qi, wt, ki_cat, k16, vt16, batch, seq, tq, tk):
    nq = seq // tq
    nk = seq // tk
    npair = N_HEADS_A // 2
    topk = min(DSA_TOPK, seq // 4)
    key_bits = max(1, (seq - 1).bit_length())
    once = pl.Buffered(1)
    qrow = lambda w: pl.BlockSpec((tq, w), lambda b, i: (b * nq + i, 0))
    return pl.pallas_call(
        functools.partial(_dsa_prompt_kernel, tq=tq, tk=tk, topk=topk, key_bits=key_bits),
        grid=(batch, nq),
        in_specs=[
            qrow(HA), qrow(HA),
            pl.BlockSpec((None, IDX_HEADS, tq), lambda b, i: (b, 0, i)),
            pl.BlockSpec((seq, 4 * IDX_DIM), lambda b, i: (b, 0), pipeline_mode=once),
            pl.BlockSpec((seq, HA), lambda b, i: (b, 0), pipeline_mode=once),
            pl.BlockSpec((None, npair, nk, LANES, tk), lambda b, i: (b, 0, 0, 0, 0), pipeline_mode=once),
        ],
        out_specs=qrow(HA),
        out_shape=jax.ShapeDtypeStruct((batch * seq, HA), BF16),
        scratch_shapes=[
            pltpu.VMEM((nk, tk, tq), I32),
            pltpu.VMEM((IDX_HEADS, tq, 4 * IDX_DIM), BF16),
            pltpu.VMEM((N_HEADS_A, tq, LANES), BF16),
            pltpu.VMEM((N_HEADS_A, 1, tq), F32),
            pltpu.VMEM((N_HEADS_A, 1, tq), F32),
            pltpu.VMEM((N_HEADS_A, LANES, tq), F32),
        ],
        compiler_params=_cparams("parallel", "parallel"),
        name="dsa_prompt",
    )(qa, qi, wt, ki_cat, k16, vt16)


def _block_mean_kernel(k_ref, o_ref):
    o_ref[...] = jnp.mean(k_ref[...], axis=0, keepdims=True)[None]


def _block_means(k2d):
    n = k2d.shape[0]
    nb = n // MOBA_BLOCK
    return pl.pallas_call(
        _block_mean_kernel,
        grid=(nb,),
        in_specs=[pl.BlockSpec((MOBA_BLOCK, HB), lambda i: (i, 0))],
        out_specs=pl.BlockSpec((1, 1, HB), lambda i: (i, 0, 0)),
        out_shape=jax.ShapeDtypeStruct((nb, 1, HB), F32),
        compiler_params=_cparams("parallel"),
        name="moba_block_means",
    )(k2d)


def _moba_prompt_kernel(q_ref, km_ref, k_ref, vt_ref, vt2_ref, o_ref,
                        sel_scr, qm_scr, m_scr, l_scr, acc_scr, *, t):
    qt = pl.program_id(2)
    q = q_ref[...]
    nb = km_ref.shape[0]
    lo_head = _pair_masks(q.shape)
    km_hi, km_lo = _split2(km_ref[...])
    blk = lax.broadcasted_iota(I32, (nb, t), 0)
    for hh in range(2):
        qh = jnp.where(lo_head if hh == 0 else ~lo_head, q, 0.0)
        q_hi, q_lo = _split2(qh)
        g = _dot3(km_hi, km_lo, q_hi, q_lo, _dot_nt)
        g = jnp.where(blk < qt, g, -jnp.inf)
        sel = jnp.zeros((nb, t), F32)
        for _ in range(MOBA_TOPK):
            mx = jnp.max(g, axis=0, keepdims=True)
            first = jnp.min(jnp.where(g == mx, blk, nb), axis=0, keepdims=True)
            pick = (blk == first) & (mx > -jnp.inf)
            sel = jnp.where(pick, 1.0, sel)
            g = jnp.where(blk == first, -jnp.inf, g)
        sel_scr[hh] = sel
    _store_pair_queries(q, qm_scr, 0)
    _softmax_init(m_scr, l_scr, acc_scr)

    def past_tile(j, _):
        k = k_ref[pl.ds(j * t, t), :]
        vt = vt_ref[j]
        for hh in range(2):
            chosen = sel_scr[hh, pl.ds(j, 1), :] > 0.0
            s = jnp.where(chosen, _dot_nt(k, qm_scr[hh]), NEG)
            _softmax_step(hh, s, vt, m_scr, l_scr, acc_scr)
        return 0

    def past_pair(jj, _):
        j = 2 * jj
        k = k_ref[pl.ds(j * t, 2 * t), :]
        vt = vt2_ref[jj]
        first = lax.broadcasted_iota(I32, (2 * t, t), 0) < t
        for hh in range(2):
            chosen = jnp.where(first, sel_scr[hh, pl.ds(j, 1), :], sel_scr[hh, pl.ds(j + 1, 1), :]) > 0.0
            s = jnp.where(chosen, _dot_nt(k, qm_scr[hh]), NEG)
            _softmax_step(hh, s, vt, m_scr, l_scr, acc_scr)
        return 0

    lax.fori_loop(0, qt // 2, past_pair, 0)

    @pl.when(qt % 2 == 1)
    def _():
        past_tile(qt - 1, 0)

    k = k_ref[pl.ds(qt * t, t), :]
    vt = vt_ref[qt]
    kidx, qidx = _key_query_index(t, t, 0, 0)
    for hh in range(2):
        s = jnp.where(kidx <= qidx, _dot_nt(k, qm_scr[hh]), NEG)
        _softmax_step(hh, s, vt, m_scr, l_scr, acc_scr)
    o_ref[...] = _pair_output(0, l_scr, acc_scr)


def _moba_prompt(q, kmean, k16, vt16, vt16_pairs, batch, seq):
    t = MOBA_T
    nq = seq // t
    nb = seq // MOBA_BLOCK
    npair = N_HEADS_B // 2
    return pl.pallas_call(
        functools.partial(_moba_prompt_kernel, t=t),
        grid=(batch, npair, nq),
        in_specs=[
            pl.BlockSpec((t, LANES), lambda b, p, i: (b * nq + i, p)),
            pl.BlockSpec((None, nb, LANES), lambda b, p, i: (b, 0, p)),
            pl.BlockSpec((seq, LANES), lambda b, p, i: (b, p)),
            pl.BlockSpec((None, None, nq, LANES, t), lambda b, p, i: (b, p, 0, 0, 0)),
            pl.BlockSpec((None, None, nq // 2, LANES, 2 * t), lambda b, p, i: (b, p, 0, 0, 0)),
        ],
        out_specs=pl.BlockSpec((t, LANES), lambda b, p, i: (b * nq + i, p)),
        out_shape=jax.ShapeDtypeStruct((batch * seq, HB), BF16),
        scratch_shapes=[
            pltpu.VMEM((2, nb, t), F32),
            pltpu.VMEM((2, t, LANES), BF16),
            pltpu.VMEM((2, 1, t), F32),
            pltpu.VMEM((2, 1, t), F32),
            pltpu.VMEM((2, LANES, t), F32),
        ],
        compiler_params=_cparams("parallel", "parallel", "parallel"),
        name="moba_prompt",
    )(q, kmean, k16, vt16, vt16_pairs)


def _fox_prompt_kernel(q_ref, cq_ref, ck_ref, k_ref, vt_ref, o_ref,
                       qm_scr, m_scr, l_scr, acc_scr, *, tq, tk):
    qt = pl.program_id(2)
    q0 = qt * tq
    _store_pair_queries(q_ref[...], qm_scr, 0)
    _softmax_init(m_scr, l_scr, acc_scr)
    cq = cq_ref[...]

    def tile(j, diagonal):
        k = k_ref[pl.ds(j * tk, tk), :]
        vt = vt_ref[j]
        ck = ck_ref[pl.ds(j * tk, tk), :]
        if diagonal:
            kidx, qidx = _key_query_index(tk, tq, j * tk, q0)
        logits = [_dot_nt(k, qm_scr[hh]) for hh in range(2)]
        for hh in range(2):
            s = logits[hh] + (cq[hh:hh + 1, :] - ck[:, hh:hh + 1])
            if diagonal:
                s = jnp.where(kidx <= qidx, s, NEG)
            _softmax_step(hh, s, vt, m_scr, l_scr, acc_scr)

    n_full = q0 // tk
    lax.fori_loop(0, n_full, lambda j, _: (tile(j, False), 0)[1], 0)
    tile(n_full, True)
    o_ref[...] = _pair_output(0, l_scr, acc_scr)


def _fox_prompt(q, cum_q, cum_k, k16, vt16, batch, seq, tq, tk):
    nq = seq // tq
    nk = seq // tk
    npair = N_HEADS_C // 2
    return pl.pallas_call(
        functools.partial(_fox_prompt_kernel, tq=tq, tk=tk),
        grid=(batch, npair, nq),
        in_specs=[
            pl.BlockSpec((tq, LANES), lambda b, p, i: (b * nq + i, p)),
            pl.BlockSpec((None, None, 2, tq), lambda b, p, i: (b, p, 0, i)),
            pl.BlockSpec((None, None, seq, 2), lambda b, p, i: (b, p, 0, 0)),
            pl.BlockSpec((seq, LANES), lambda b, p, i: (b, p)),
            pl.BlockSpec((None, None, nk, LANES, tk), lambda b, p, i: (b, p, 0, 0, 0)),
        ],
        out_specs=pl.BlockSpec((tq, LANES), lambda b, p, i: (b * nq + i, p)),
        out_shape=jax.ShapeDtypeStruct((batch * seq, HC), BF16),
        scratch_shapes=[
            pltpu.VMEM((2, tq, LANES), BF16),
            pltpu.VMEM((2, 1, tq), F32),
            pltpu.VMEM((2, 1, tq), F32),
            pltpu.VMEM((2, LANES, tq), F32),
        ],
        compiler_params=_cparams("parallel", "parallel", "parallel"),
        name="fox_prompt",
    )(q, cum_q, cum_k, k16, vt16)


def _router_kernel(x_ref, rwt_ref, rb_ref, g_ref):
    x_hi, x_lo = _split2(x_ref[...])
    rw_hi, rw_lo = _split2(rwt_ref[...])
    logits = _dot3(rw_hi, rw_lo, x_hi, x_lo, _dot_nt)
    scores = 1.0 / (1.0 + jnp.exp(-logits))
    biased = scores + rb_ref[...]
    rows = [biased[e:e + 1, :] for e in range(N_EXPERTS)]
    ninf = jnp.full_like(rows[0], -jnp.inf)

    def top2(vals):
        mx = functools.reduce(jnp.maximum, vals)
        picks1, found = [], jnp.zeros_like(mx) > 0
        for vv in vals:
            p = (vv == mx) & ~found
            found = found | p
            picks1.append(p)
        rest = [jnp.where(p, ninf, vv) for p, vv in zip(picks1, vals)]
        mx2 = functools.reduce(jnp.maximum, rest)
        picks2, found = [], jnp.zeros_like(mx) > 0
        for vv in rest:
            p = (vv == mx2) & ~found
            found = found | p
            picks2.append(p)
        return mx, mx2, picks1, picks2

    grp_score = []
    for gi in range(N_GROUPS):
        m1, m2, _, _ = top2(rows[gi * EXPERTS_PER_GROUP:(gi + 1) * EXPERTS_PER_GROUP])
        grp_score.append(m1 + m2)
    best = grp_score[0]
    g_sel = jnp.zeros_like(best, dtype=I32)
    for gi in range(1, N_GROUPS):
        better = grp_score[gi] > best
        best = jnp.where(better, grp_score[gi], best)
        g_sel = jnp.where(better, gi, g_sel)
    masked = [jnp.where(g_sel == e // EXPERTS_PER_GROUP, rows[e], ninf) for e in range(N_EXPERTS)]
    _, _, p1, p2 = top2(masked)
    zero = jnp.zeros_like(best)
    w1 = functools.reduce(jnp.add, [jnp.where(p1[e], scores[e:e + 1, :], zero) for e in range(N_EXPERTS)])
    w2 = functools.reduce(jnp.add, [jnp.where(p2[e], scores[e:e + 1, :], zero) for e in range(N_EXPERTS)])
    tot = w1 + w2
    for e in range(N_EXPERTS):
        g_ref[e:e + 1, :] = jnp.where(p1[e], w1 / tot, zero) + jnp.where(p2[e], w2 / tot, zero)


def _router(x2d, rwt, rb, tm):
    n = x2d.shape[0]
    full = lambda a: pl.BlockSpec(a.shape, lambda i: (0, 0))
    return pl.pallas_call(
        _router_kernel,
        grid=(n // tm,),
        in_specs=[pl.BlockSpec((tm, D_MODEL), lambda i: (i, 0)), full(rwt), full(rb)],
        out_specs=pl.BlockSpec((N_EXPERTS, tm), lambda i: (0, i)),
        out_shape=jax.ShapeDtypeStruct((N_EXPERTS, n), F32),
        compiler_params=_cparams("parallel"),
        name="moe_router",
    )(x2d, rwt, rb)


def _moe_kernel(x_ref, gates_ref, wg_ref, wu_ref, wd_ref, g_ref, b_ref, y_ref, xb_scr, acc_scr):
    e = pl.program_id(1)

    @pl.when(e == 0)
    def _():
        for i, part in enumerate(_lhs(x_ref[...], wg_ref)):
            xb_scr[i] = part
        acc_scr[...] = jnp.zeros_like(acc_scr)

    xs = tuple(xb_scr[i] for i in range(xb_scr.shape[0]))
    gate = _mm(xs, wg_ref[0])
    up = _mm(xs, wu_ref[0])
    h = gate * (1.0 / (1.0 + jnp.exp(-gate))) * up
    down = _mm(_lhs(h, wd_ref), wd_ref[0])
    gates = gates_ref[...]
    lane = lax.broadcasted_iota(I32, gates.shape, 1)
    w = jnp.sum(jnp.where(lane == e, gates, 0.0), axis=1, keepdims=True)
    acc_scr[...] += w * down

    @pl.when(e == pl.num_programs(1) - 1)
    def _():
        y_ref[...] = _layer_norm(ALPHA * x_ref[...] + acc_scr[...], g_ref[...], b_ref[...])


def _moe_ln(x2d, gates, wg16, wu16, wd16, g, b, tm):
    n = x2d.shape[0]
    full = lambda a: pl.BlockSpec(a.shape, lambda i, e: (0, 0))
    return pl.pallas_call(
        _moe_kernel,
        grid=(n // tm, N_EXPERTS),
        in_specs=[
            pl.BlockSpec((tm, D_MODEL), lambda i, e: (i, 0)),
            pl.BlockSpec((tm, N_EXPERTS), lambda i, e: (i, 0)),
            pl.BlockSpec((1, D_MODEL, D_EXPERT), lambda i, e: (e, 0, 0)),
            pl.BlockSpec((1, D_MODEL, D_EXPERT), lambda i, e: (e, 0, 0)),
            pl.BlockSpec((1, D_EXPERT, D_MODEL), lambda i, e: (e, 0, 0)),
            full(g), full(b),
        ],
        out_specs=pl.BlockSpec((tm, D_MODEL), lambda i, e: (i, 0)),
        out_shape=jax.ShapeDtypeStruct((n, D_MODEL), F32),
        scratch_shapes=[pltpu.VMEM((2 if wg16.dtype == F32 else 1, tm, D_MODEL), BF16),
                        pltpu.VMEM((tm, D_MODEL), F32)],
        compiler_params=_cparams("parallel", "arbitrary"),
        name="moe_experts_ln",
    )(x2d, gates, wg16, wu16, wd16, g, b)


def _page_view(cache):
    return jnp.moveaxis(cache, 1, -1)


def _one_key_page(col):
    shape = col.shape[:-1] + (PAGE_SIZE,)
    lane = lax.broadcasted_iota(I32, shape, len(shape) - 1)
    return jnp.where(lane == 0, col, 0.0)


def _new_key_bias(bias_new):
    lane = lax.broadcasted_iota(I32, (bias_new.shape[0], PAGE_SIZE), 1)
    return jnp.where(lane == 0, bias_new, NEG)


def _decode_logits(q_ref, k_refs, knew_ref, s_scr):
    n_heads = q_ref.shape[0]
    qb = jnp.broadcast_to(q_ref[...] * QK_SCALE, (n_heads, HEAD_DIM, PAGE_SIZE))
    for p, ref in enumerate(k_refs):
        s_scr[p] = jnp.sum(ref[...] * qb, axis=1)
    s_scr[len(k_refs)] = jnp.sum(_one_key_page(knew_ref[...]) * qb, axis=1)


def _decode_attend(s_scr, v_refs, vnew_ref, o_ref):
    s = s_scr[...]
    m = jnp.max(jnp.max(s, axis=0), axis=1, keepdims=True)
    s_scr[...] = jnp.exp(s - m)
    l = jnp.sum(jnp.sum(s_scr[...], axis=0), axis=1, keepdims=True)
    acc = _one_key_page(vnew_ref[...]) * s_scr[len(v_refs)][:, None, :]
    for p, ref in enumerate(v_refs):
        acc = acc + ref[...] * s_scr[p][:, None, :]
    o_ref[...] = jnp.sum(acc, axis=2, keepdims=True) / l[:, :, None]


def _dsa_sample_score_kernel(pt_ref, q_ref, w_ref, knew_ref, *rest, n_pages):
    page_refs = rest[:n_pages]
    s_ref = rest[n_pages]
    q = q_ref[...]
    q_hi, q_lo = _split2(q)
    w = w_ref[...]
    for p in range(n_pages):
        k_hi, k_lo = _split2(page_refs[p][...])
        d = _dot3(q_hi, q_lo, k_hi, k_lo, _dot)
        s_ref[p:p + 1, :] = jnp.sum(w * jnp.maximum(d, 0.0), axis=0, keepdims=True) * IDX_SCALE
    k_new = knew_ref[...][:, 0:IDX_DIM]
    d_new = jnp.sum(q * k_new, axis=1, keepdims=True)
    s_new = jnp.sum(w * jnp.maximum(d_new, 0.0), axis=0, keepdims=True) * IDX_SCALE
    lane = lax.broadcasted_iota(I32, (1, PAGE_SIZE), 1)
    s_ref[n_pages:n_pages + 1, :] = jnp.where(lane == 0, s_new, -jnp.inf)


def _dsa_sample_scores(page_table, q_i, w_i, tail, idx_view):
    nseq, n_pages = page_table.shape
    page = lambda p: pl.BlockSpec((None, IDX_DIM, PAGE_SIZE), lambda b, pt: (pt[b, p], 0, 0))
    grid_spec = pltpu.PrefetchScalarGridSpec(
        num_scalar_prefetch=1,
        grid=(nseq,),
        in_specs=[
            pl.BlockSpec((None, IDX_HEADS, IDX_DIM), lambda b, pt: (b, 0, 0)),
            pl.BlockSpec((None, IDX_HEADS, 1), lambda b, pt: (b, 0, 0)),
            pl.BlockSpec((None, 1, LANES), lambda b, pt: (b, 0, 0)),
        ] + [page(p) for p in range(n_pages)],
        out_specs=pl.BlockSpec((None, n_pages + 1, PAGE_SIZE), lambda b, pt: (b, 0, 0)),
    )
    return pl.pallas_call(
        functools.partial(_dsa_sample_score_kernel, n_pages=n_pages),
        grid_spec=grid_spec,
        out_shape=jax.ShapeDtypeStruct((nseq, n_pages + 1, PAGE_SIZE), F32),
        compiler_params=_cparams("parallel"),
        name="dsa_sample_scores",
    )(page_table, q_i, w_i, tail, *([idx_view] * n_pages))


def _select_kernel(s_ref, valid_ref, bias_ref, key_scr, *, topk, key_bits):
    n_tiles, tk, nseq = s_ref.shape
    for j in range(n_tiles):
        key_scr[j] = jnp.where(valid_ref[j] > 0.0, _sortable_key(s_ref[j]), INT_MIN)
    t, c = _topk_threshold(key_scr, n_tiles, topk, key_bits)
    for j in range(n_tiles):
        kt = key_scr[j]
        kidx = j * tk + lax.broadcasted_iota(I32, kt.shape, 0)
        bias_ref[j] = jnp.where((kt > t) | ((kt == t) & (kidx <= c)), 0.0, NEG)


def _select_topk_bias(scores_t, valid_t, topk):
    n_chunks = scores_t.shape[0]
    key_bits = max(1, (n_chunks * PAGE_SIZE - 1).bit_length())
    return pl.pallas_call(
        functools.partial(_select_kernel, topk=topk, key_bits=key_bits),
        out_shape=jax.ShapeDtypeStruct(scores_t.shape, F32),
        scratch_shapes=[pltpu.VMEM(scores_t.shape, I32)],
        compiler_params=pltpu.CompilerParams(vmem_limit_bytes=VMEM_LIMIT),
        name="dsa_sample_select",
    )(scores_t, valid_t)


def _decode_call(kernel_fn, name, page_table, n_heads, head_inputs, extra_inputs, extra_specs, k_view, v_view):
    nseq, n_pages = page_table.shape
    hd = pl.BlockSpec((None, n_heads, HEAD_DIM, 1), lambda b, pt: (b, 0, 0, 0))
    page = lambda p: pl.BlockSpec((None, n_heads, HEAD_DIM, PAGE_SIZE), lambda b, pt: (pt[b, p], 0, 0, 0))
    pages = [page(p) for p in range(n_pages)]
    grid_spec = pltpu.PrefetchScalarGridSpec(
        num_scalar_prefetch=1,
        grid=(nseq,),
        in_specs=[hd] * len(head_inputs) + list(extra_specs) + pages + pages,
        out_specs=hd,
        scratch_shapes=[pltpu.VMEM((n_pages + 1, n_heads, PAGE_SIZE), F32)],
    )
    return pl.pallas_call(
        functools.partial(kernel_fn, n_pages=n_pages),
        grid_spec=grid_spec,
        out_shape=jax.ShapeDtypeStruct((nseq, n_heads, HEAD_DIM, 1), F32),
        compiler_params=_cparams("parallel"),
        name=name,
    )(page_table, *head_inputs, *extra_inputs, *([k_view] * n_pages), *([v_view] * n_pages))


def _dsa_sample_attn_kernel(pt_ref, q_ref, knew_ref, vnew_ref, bias_ref, *rest, n_pages):
    k_refs, v_refs = rest[:n_pages], rest[n_pages:2 * n_pages]
    o_ref, s_scr = rest[2 * n_pages], rest[2 * n_pages + 1]
    _decode_logits(q_ref, k_refs, knew_ref, s_scr)
    s_scr[...] = s_scr[...] + bias_ref[...][:, None, :]
    _decode_attend(s_scr, v_refs, vnew_ref, o_ref)


def _dsa_sample_attn(page_table, q, k_new, v_new, bias, k_view, v_view):
    n_pages = page_table.shape[1]
    spec = pl.BlockSpec((None, n_pages + 1, PAGE_SIZE), lambda b, pt: (b, 0, 0))
    return _decode_call(_dsa_sample_attn_kernel, "dsa_sample_attn", page_table, N_HEADS_A,
                        (q, k_new, v_new), (bias,), (spec,), k_view, v_view)


def _moba_sample_kernel(pt_ref, q_ref, knew_ref, vnew_ref, *rest, n_pages):
    k_refs, v_refs = rest[:n_pages], rest[n_pages:2 * n_pages]
    o_ref, s_scr = rest[2 * n_pages], rest[2 * n_pages + 1]
    ppb = MOBA_BLOCK // PAGE_SIZE
    n_blocks = n_pages // ppb
    q = q_ref[...]
    gates = []
    for n in range(n_blocks):
        ksum = functools.reduce(jnp.add, [k_refs[n * ppb + i][...] for i in range(ppb)])
        k_mean = jnp.sum(ksum, axis=2, keepdims=True) / float(MOBA_BLOCK)
        gates.append(jnp.sum(q * k_mean, axis=1))
    chosen = [jnp.zeros_like(gates[0]) > 0 for _ in range(n_blocks)]
    for _ in range(min(MOBA_TOPK, n_blocks + 1)):
        mx = functools.reduce(jnp.maximum, gates)
        found = jnp.zeros_like(mx) > 0
        for n in range(n_blocks):
            pick = (gates[n] == mx) & ~found & (mx > -jnp.inf)
            found = found | pick
            chosen[n] = chosen[n] | pick
            gates[n] = jnp.where(pick, -jnp.inf, gates[n])
    _decode_logits(q_ref, k_refs, knew_ref, s_scr)
    for p in range(n_pages):
        s_scr[p] = s_scr[p] + jnp.where(chosen[p // ppb], 0.0, NEG)
    s_scr[n_pages] = s_scr[n_pages] + _new_key_bias(jnp.zeros((1, 1), F32))
    _decode_attend(s_scr, v_refs, vnew_ref, o_ref)


def _moba_sample(page_table, q, k_new, v_new, k_view, v_view):
    return _decode_call(_moba_sample_kernel, "moba_sample", page_table, N_HEADS_B,
                        (q, k_new, v_new), (), (), k_view, v_view)


def _fox_sample_kernel(pt_ref, q_ref, knew_ref, vnew_ref, lnew_ref, *rest, n_pages):
    l_refs = rest[:n_pages]
    k_refs, v_refs = rest[n_pages:2 * n_pages], rest[2 * n_pages:3 * n_pages]
    o_ref, s_scr = rest[3 * n_pages], rest[3 * n_pages + 1]
    r = lax.broadcasted_iota(I32, (PAGE_SIZE, PAGE_SIZE), 0)
    cc = lax.broadcasted_iota(I32, (PAGE_SIZE, PAGE_SIZE), 1)
    upper = jnp.where(r <= cc, 1.0, 0.0).astype(BF16)
    carry = jnp.zeros((N_HEADS_C, 1), F32)
    cums = []
    for p in range(n_pages):
        hi, mid, lo = _split3(l_refs[p][...])
        cum = (_dot(hi, upper) + _dot(mid, upper) + _dot(lo, upper)) + carry
        cums.append(cum)
        carry = cum[:, PAGE_SIZE - 1:PAGE_SIZE]
    cum_q = carry + lnew_ref[...]
    _decode_logits(q_ref, k_refs, knew_ref, s_scr)
    for p in range(n_pages):
        s_scr[p] = s_scr[p] + (cum_q - cums[p])
    s_scr[n_pages] = s_scr[n_pages] + _new_key_bias(cum_q - cum_q)
    _decode_attend(s_scr, v_refs, vnew_ref, o_ref)


def _fox_sample(page_table, q, k_new, v_new, logf_new, logf_view, k_view, v_view):
    n_pages = page_table.shape[1]
    lnew = pl.BlockSpec((None, N_HEADS_C, 1), lambda b, pt: (b, 0, 0))
    lpage = lambda p: pl.BlockSpec((None, N_HEADS_C, PAGE_SIZE), lambda b, pt: (pt[b, p], 0, 0))
    return _decode_call(_fox_sample_kernel, "fox_sample", page_table, N_HEADS_C,
                        (q, k_new, v_new), (logf_new,) + (logf_view,) * n_pages,
                        (lnew,) + tuple(lpage(p) for p in range(n_pages)), k_view, v_view)


def _pad_cols(w, width):
    return jnp.pad(w, ((0, 0), (0, width - w.shape[1])))


def _rope_inv_freq():
    half = HEAD_DIM // 8
    inv = ROPE_THETA ** (-jnp.arange(half, dtype=F32) / half)
    per_head = jnp.concatenate([inv, inv, jnp.zeros((HEAD_DIM - 2 * half,), F32)])
    return jnp.tile(per_head, LANES // HEAD_DIM)[None, :]


def _transposed_values(v16, batch, seq, tk):
    npair = v16.shape[1] // LANES
    return v16.reshape(batch, seq // tk, tk, npair, LANES).transpose(0, 3, 1, 4, 2)


def _ab_layer_prompt(x2d, w16, w_out16, invf, g, b, batch, seq):
    pos = jnp.tile(jnp.arange(seq, dtype=F32), batch)[:, None]
    qa, ka, va, qb, kb, vb, qi, tail, ka16, va16, kb16, vb16, ki_cat = _ab_project(x2d, w16, pos, invf, PROJ_TM)
    k_idx = tail[:, :IDX_DIM]
    wt = tail[:, IDX_DIM:IDX_DIM + IDX_HEADS].reshape(batch, seq, IDX_HEADS).transpose(0, 2, 1)
    o_a = _dsa_prompt(qa, qi, wt, ki_cat, ka16, _transposed_values(va16, batch, seq, DSA_TK),
                      batch, seq, DSA_TQ, DSA_TK)
    kmean = _block_means(kb).reshape(batch, seq // MOBA_BLOCK, HB)
    o_b = _moba_prompt(qb, kmean, kb16, _transposed_values(vb16, batch, seq, MOBA_T),
                       _transposed_values(vb16, batch, seq, 2 * MOBA_T), batch, seq)
    o = jnp.concatenate([o_a, o_b], axis=1)
    y = _out_proj_ln(o, w_out16, x2d, g, b, PROJ_TM)
    return y, (ka, va, k_idx, kb, vb)


def _ab_layer_sample(x2d, w16, w_out16, invf, g, b, past_len, page_table,
                     cache_a_k, cache_a_v, cache_a_idx, cache_b_k, cache_b_v):
    nseq = x2d.shape[0]
    n_pages = page_table.shape[1]
    pos = jnp.full((nseq, 1), past_len, F32)
    qa, ka, va, qb, kb, vb, qi, tail = _ab_project(x2d, w16, pos, invf, nseq)[:8]
    k_idx = tail[:, :IDX_DIM]
    w_i = tail[:, IDX_DIM:IDX_DIM + IDX_HEADS]
    col = lambda a: a.reshape(nseq, -1, HEAD_DIM, 1)
    scores = _dsa_sample_scores(page_table, qi.reshape(nseq, IDX_HEADS, IDX_DIM), w_i.reshape(nseq, IDX_HEADS, 1),
                                tail.reshape(nseq, 1, LANES), _page_view(cache_a_idx))
    n_keys = n_pages * PAGE_SIZE + 1
    valid = (jnp.arange((n_pages + 1) * PAGE_SIZE) < n_keys).astype(F32)
    valid_t = jnp.broadcast_to(valid.reshape(n_pages + 1, PAGE_SIZE, 1), (n_pages + 1, PAGE_SIZE, nseq))
    bias_t = _select_topk_bias(scores.transpose(1, 2, 0), valid_t, min(DSA_TOPK, n_keys // 4))
    o_a = _dsa_sample_attn(page_table, col(qa), col(ka), col(va), bias_t.transpose(2, 0, 1),
                           _page_view(cache_a_k), _page_view(cache_a_v))
    o_b = _moba_sample(page_table, col(qb), col(kb), col(vb), _page_view(cache_b_k), _page_view(cache_b_v))
    o = jnp.concatenate([o_a.reshape(nseq, HA), o_b.reshape(nseq, HB)], axis=1)
    y = _out_proj_ln(o, w_out16, x2d, g, b, nseq)
    return y, (ka, va, k_idx, kb, vb)


def _fox_layer_prompt(x2d, w16, bf_pad, w_out16, g, b, batch, seq):
    q, k, v, logf, cum, k16, v16 = _fox_project(x2d, w16, bf_pad, PROJ_TM, seq)
    npair = N_HEADS_C // 2
    cum_k = cum[:, :N_HEADS_C].reshape(batch, seq, npair, 2).transpose(0, 2, 1, 3)
    cum_q = cum_k.transpose(0, 1, 3, 2)
    o = _fox_prompt(q, cum_q, cum_k, k16, _transposed_values(v16, batch, seq, FOX_TK),
                    batch, seq, FOX_TQ, FOX_TK)
    y = _out_proj_ln(o, w_out16, x2d, g, b, PROJ_TM)
    return y, (k, v, logf[:, :N_HEADS_C])


def _fox_layer_sample(x2d, w16, bf_pad, w_out16, g, b, page_table, cache_c_k, cache_c_v, cache_c_logf):
    nseq = x2d.shape[0]
    q, k, v, logf, _, _, _ = _fox_project(x2d, w16, bf_pad, nseq, nseq)
    col = lambda a: a.reshape(nseq, N_HEADS_C, HEAD_DIM, 1)
    o = _fox_sample(page_table, col(q), col(k), col(v), logf[:, :N_HEADS_C].reshape(nseq, N_HEADS_C, 1),
                    _page_view(cache_c_logf), _page_view(cache_c_k), _page_view(cache_c_v))
    y = _out_proj_ln(o.reshape(nseq, HC), w_out16, x2d, g, b, nseq)
    return y, (k, v, logf[:, :N_HEADS_C])


def _moe_layer(x2d, rwt, rb, wg16, wu16, wd16, g, b, tm):
    gates_t = _router(x2d, rwt, rb, tm)
    return _moe_ln(x2d, gates_t.T, wg16, wu16, wd16, g, b, tm)


def kernel(x_prompt, x_sample, cache_a_k, cache_a_v, cache_a_idx, cache_b_k, cache_b_v, cache_c_k, cache_c_v, cache_c_logf, page_table, w_in_ab, w_out_ab, w_in_fox, b_forget, w_out_fox, ln_mix_g, ln_mix_b, ln_ffn_g, ln_ffn_b, router_w, router_bias, exp_w_gate, exp_w_up, exp_w_down):
    batch, seq, _ = x_prompt.shape
    nseq = x_sample.shape[0]
    past_len = page_table.shape[1] * PAGE_SIZE

    w_ab32 = _pad_cols(w_in_ab, 7 * HA + LANES)
    w_fox32 = _pad_cols(w_in_fox, 3 * HC + LANES)
    w_ab16, w_out_ab16 = w_ab32.astype(BF16), w_out_ab.astype(BF16)
    w_fox16, w_out_fox16 = w_fox32.astype(BF16), w_out_fox.astype(BF16)
    bf_pad = jnp.pad(b_forget, (0, LANES - N_HEADS_C))[None, :]
    invf = _rope_inv_freq()
    rwt = router_w.T
    rb = router_bias[:, None]
    wg16, wu16, wd16 = exp_w_gate.astype(BF16), exp_w_up.astype(BF16), exp_w_down.astype(BF16)
    row = lambda a, i: a[i][None, :]

    xp = x_prompt.reshape(batch * seq, D_MODEL)
    xs = x_sample.reshape(nseq, D_MODEL)

    xp, (pa_k, pa_v, pa_idx, pb_k, pb_v) = _ab_layer_prompt(
        xp, w_ab16, w_out_ab16, invf, row(ln_mix_g, 0), row(ln_mix_b, 0), batch, seq)
    xs, (sa_k, sa_v, sa_idx, sb_k, sb_v) = _ab_layer_sample(
        xs, w_ab32, w_out_ab, invf, row(ln_mix_g, 0), row(ln_mix_b, 0), past_len, page_table,
        cache_a_k, cache_a_v, cache_a_idx, cache_b_k, cache_b_v)
    xp = _moe_layer(xp, rwt, rb, wg16[0], wu16[0], wd16[0], row(ln_ffn_g, 0), row(ln_ffn_b, 0), MOE_TM)
    xs = _moe_layer(xs, rwt, rb, exp_w_gate[0], exp_w_up[0], exp_w_down[0],
                    row(ln_ffn_g, 0), row(ln_ffn_b, 0), nseq)

    xp, (pc_k, pc_v, pc_logf) = _fox_layer_prompt(
        xp, w_fox16, bf_pad, w_out_fox16, row(ln_mix_g, 1), row(ln_mix_b, 1), batch, seq)
    xs, (sc_k, sc_v, sc_logf) = _fox_layer_sample(
        xs, w_fox32, bf_pad, w_out_fox, row(ln_mix_g, 1), row(ln_mix_b, 1), page_table,
        cache_c_k, cache_c_v, cache_c_logf)
    xp = _moe_layer(xp, rwt, rb, wg16[1], wu16[1], wd16[1], row(ln_ffn_g, 1), row(ln_ffn_b, 1), MOE_TM)
    xs = _moe_layer(xs, rwt, rb, exp_w_gate[1], exp_w_up[1], exp_w_down[1],
                    row(ln_ffn_g, 1), row(ln_ffn_b, 1), nseq)

    hd = lambda a, nh, lead: a.reshape(*lead, nh, HEAD_DIM)
    lp, ls = (batch, seq), (nseq, 1)
    return (xp.reshape(batch, seq, D_MODEL), xs.reshape(nseq, 1, D_MODEL),
            hd(pa_k, N_HEADS_A, lp), hd(pa_v, N_HEADS_A, lp), pa_idx.reshape(batch, seq, IDX_DIM),
            hd(pb_k, N_HEADS_B, lp), hd(pb_v, N_HEADS_B, lp),
            hd(pc_k, N_HEADS_C, lp), hd(pc_v, N_HEADS_C, lp), pc_logf.reshape(batch, seq, N_HEADS_C),
            hd(sa_k, N_HEADS_A, ls), hd(sa_v, N_HEADS_A, ls), sa_idx.reshape(nseq, 1, IDX_DIM),
            hd(sb_k, N_HEADS_B, ls), hd(sb_v, N_HEADS_B, ls),
            hd(sc_k, N_HEADS_C, ls), hd(sc_v, N_HEADS_C, ls), sc_logf.reshape(nseq, 1, N_HEADS_C))
```
